```python
import jax, jax.numpy as jnp
from jax import lax
import numpy as np

D_MODEL = 1024
BATCH = 4
SEQ = 4096
DEPTH = 1
DEC_BATCH = 32
DEC_SEQ = 2048
PAST_LEN = 128

HEAD_DIM = 64
N_HEADS = D_MODEL // HEAD_DIM
NA_HEADS = N_HEADS // 2
DIL_HEADS = N_HEADS - NA_HEADS
NA_WIDTH = NA_HEADS * HEAD_DIM
DIL_WIDTH = DIL_HEADS * HEAD_DIM
QKV_WIDTH = 3 * (NA_WIDTH + DIL_WIDTH)
D_FF = -(-8 * D_MODEL // (3 * 256)) * 256
GRID_W = 64
NA_ROWS = 8
NA_COLS = 16
NA_KEY_COLS = 2 * NA_COLS
DIL_PATTERNS = ((128, 1), (512, 4), (2048, 16))
RMS_EPS = 1e-6
NEG = -1e30

kernel_name = "hymba_style_natten_dilated_encoder"


def rms_norm(x, g):
    xf = x.astype(jnp.float32)
    y = xf * lax.rsqrt(jnp.mean(xf * xf, axis=-1, keepdims=True) + RMS_EPS)
    return (y * g.astype(jnp.float32)).astype(x.dtype)


def alibi_slopes(n):
    return np.array([2.0 ** (-8.0 * (i + 1) / n) for i in range(n)], dtype=np.float32)


def neighbourhood_attention(q, k, v, rpb):
    b, t, h, dh = q.shape
    rows = t // GRID_W
    kh = min(NA_ROWS, rows)
    nblk = GRID_W // NA_COLS
    qcol = np.arange(GRID_W).reshape(nblk, NA_COLS)
    kstart = np.clip(np.arange(nblk) * NA_COLS - NA_COLS // 2, 0, GRID_W - NA_KEY_COLS)
    kcol = kstart[:, None] + np.arange(NA_KEY_COLS)
    wstart = np.clip(qcol - NA_COLS // 2, 0, GRID_W - NA_COLS)
    kc3 = kcol[:, None, :]
    col_ok = (kc3 >= wstart[..., None]) & (kc3 < wstart[..., None] + NA_COLS)
    rel_c_idx = np.clip(kc3 - qcol[..., None] + NA_COLS - 1, 0, 2 * NA_COLS - 2)
    col_bias = rpb.astype(jnp.float32)[:, :, rel_c_idx]
    mask = jnp.asarray(col_ok)[:, None, :, None, :]

    qg = (q * (dh ** -0.5)).reshape(b, rows, nblk, NA_COLS, h, dh).transpose(1, 0, 2, 3, 4, 5)
    kg = k.reshape(b, rows, GRID_W, h, dh)[:, :, kcol]
    vg = v.reshape(b, rows, GRID_W, h, dh)[:, :, kcol]

    def row_fn(args):
        r, q_r = args
        rs = jnp.clip(r - kh // 2, 0, rows - kh)
        k_r = lax.dynamic_slice_in_dim(kg, rs, kh, axis=1)
        v_r = lax.dynamic_slice_in_dim(vg, rs, kh, axis=1)
        roff = rs + jnp.arange(kh) - r + NA_ROWS - 1
        bias = col_bias[:, roff].transpose(2, 0, 3, 1, 4)
        s = jnp.einsum('bnqhd,bknjhd->bnhqkj', q_r, k_r).astype(jnp.float32)
        s = jnp.where(mask, s + bias, NEG)
        p = jax.nn.softmax(s, axis=(-2, -1))
        o = jnp.einsum('bnhqkj,bknjhd->bnqhd', p.astype(v.dtype), v_r)
        return o.reshape(b, GRID_W, h, dh)

    out = lax.map(row_fn, (jnp.arange(rows), qg))
    return out.transpose(1, 0, 2, 3, 4).reshape(b, t, h, dh)


def dilated_branch(q, k, v, slopes, window, dilation):
    b, t, h, dh = q.shape
    half = window // (2 * dilation)
    L = t // dilation
    c = half
    nb = -(-L // c)
    lp = nb * c

    def phase(x):
        return x.reshape(b, L, dilation, h, dh).transpose(0, 2, 1, 3, 4)

    def band(x):
        xp = jnp.pad(phase(x), ((0, 0), (0, 0), (c, lp - L + c), (0, 0), (0, 0)))
        xp = xp.reshape(b, dilation, nb + 2, c, h, dh)
        return jnp.concatenate([xp[:, :, :-2], xp[:, :, 1:-1], xp[:, :, 2:]], axis=3)

    qp = jnp.pad(phase(q * (dh ** -0.5)), ((0, 0), (0, 0), (0, lp - L), (0, 0), (0, 0)))
    qp = qp.reshape(b, dilation, nb, c, h, dh)
    kb, vb = band(k), band(v)
    diff = np.arange(3 * c)[None, :] - c - np.arange(c)[:, None]
    lk = np.arange(nb)[:, None] * c - c + np.arange(3 * c)[None, :]
    valid = (np.abs(diff)[None] <= half) & (lk[:, None, :] >= 0) & (lk[:, None, :] < L)
    bias = -slopes[:, None, None] * jnp.asarray((dilation * np.abs(diff)).astype(np.float32))
    s = jnp.einsum('bpnqhd,bpnkhd->bpnhqk', qp, kb).astype(jnp.float32)
    s = jnp.where(jnp.asarray(valid)[:, None], s + bias, NEG)
    m = s.max(-1)
    e = jnp.exp(s - m[..., None])
    den = e.sum(-1)
    num = jnp.einsum('bpnhqk,bpnkhd->bpnqhd', e, vb.astype(jnp.float32))

    def unphase(x):
        x = x.reshape((b, dilation, lp) + x.shape[4:])[:, :, :L]
        x = jnp.swapaxes(x, 1, 2)
        return x.reshape((b, t) + x.shape[3:])

    return (unphase(m.transpose(0, 1, 2, 4, 3)), unphase(num), unphase(den.transpose(0, 1, 2, 4, 3)))


def dilated_attention(q, k, v, slopes):
    branches = [dilated_branch(q, k, v, slopes, w, d) for (w, d) in DIL_PATTERNS]
    m_all = jnp.stack([br[0] for br in branches])
    wts = jnp.exp(m_all - m_all.max(0))
    num = sum(wts[i][..., None] * branches[i][1] for i in range(len(branches)))
    den = sum(wts[i] * branches[i][2] for i in range(len(branches)))
    return num / den[..., None]


def encoder_layer(x, w_in, rpb, g_attn, g_na, g_dil, w_out, g_ffn, w_gate, w_up, w_down):
    b, t, _ = x.shape
    hn = rms_norm(x, g_attn)
    proj = hn @ w_in
    qa, ka, va, qd, kd, vd = jnp.split(proj, 6, axis=-1)
    heads = lambda z: z.reshape(b, t, -1, HEAD_DIM)
    slopes = jnp.asarray(alibi_slopes(DIL_HEADS))
    oa = neighbourhood_attention(heads(qa), heads(ka), heads(va), rpb).reshape(b, t, NA_WIDTH)
    od = dilated_attention(heads(qd), heads(kd), heads(vd), slopes).astype(x.dtype).reshape(b, t, DIL_WIDTH)
    mix = jnp.concatenate([rms_norm(oa, g_na), rms_norm(od, g_dil)], axis=-1)
    x = x + mix @ w_out
    hn = rms_norm(x, g_ffn)
    return x + (jax.nn.silu(hn @ w_gate) * (hn @ w_up)) @ w_down


def setup_inputs(seed: int = 0) -> dict:
    key = jax.random.key(seed)
    ks = jax.random.split(key, 14)
    nrm = lambda k_, shp, sc: jax.random.normal(k_, shp, jnp.float32) * sc
    gain = lambda k_, n: 1.0 + 0.01 * jax.random.normal(k_, (DEPTH, n), jnp.float32)
    return {
        "x_prompt": jax.random.normal(ks[0], (BATCH, SEQ, D_MODEL), jnp.float32),
        "x_sample": jax.random.normal(ks[1], (DEC_BATCH, DEC_SEQ, D_MODEL), jnp.float32),
        "w_in": nrm(ks[2], (DEPTH, D_MODEL, QKV_WIDTH), D_MODEL ** -0.5),
        "rpb": nrm(ks[3], (DEPTH, NA_HEADS, 2 * NA_ROWS - 1, 2 * NA_COLS - 1), 0.02),
        "g_attn": gain(ks[4], D_MODEL),
        "g_na": gain(ks[5], NA_WIDTH),
        "g_dil": gain(ks[6], DIL_WIDTH),
        "w_out": nrm(ks[7], (DEPTH, D_MODEL, D_MODEL), D_MODEL ** -0.5),
        "g_ffn": gain(ks[8], D_MODEL),
        "w_gate": nrm(ks[9], (DEPTH, D_MODEL, D_FF), D_MODEL ** -0.5),
        "w_up": nrm(ks[10], (DEPTH, D_MODEL, D_FF), D_MODEL ** -0.5),
        "w_down": nrm(ks[11], (DEPTH, D_FF, D_MODEL), D_FF ** -0.5),
        "g_final": 1.0 + 0.01 * jax.random.normal(ks[12], (D_MODEL,), jnp.float32),
    }


def reference(x_prompt, x_sample, w_in, rpb, g_attn, g_na, g_dil, w_out, g_ffn, w_gate, w_up, w_down, g_final):
    def trunk(x):
        for l in range(DEPTH):
            x = encoder_layer(x, w_in[l], rpb[l], g_attn[l], g_na[l], g_dil[l], w_out[l],
                              g_ffn[l], w_gate[l], w_up[l], w_down[l])
        return rms_norm(x, g_final)

    y_prompt = trunk(x_prompt)
    y_sample = trunk(x_sample)
    return (y_prompt, y_sample)
```

```python
import functools

import jax
import jax.numpy as jnp
import numpy as np
from jax import lax
from jax.experimental import pallas as pl
from jax.experimental.pallas import tpu as pltpu

D_MODEL = 1024
HEAD_DIM = 64
NA_HEADS = 8
DIL_HEADS = 8
NA_WIDTH = NA_HEADS * HEAD_DIM
DIL_WIDTH = DIL_HEADS * HEAD_DIM
QKV_WIDTH = 3 * (NA_WIDTH + DIL_WIDTH)
DIL_QKV = 3 * DIL_WIDTH
D_FF = 2816
GRID_W = 64
NA_ROWS = 8
NA_COLS = 16
DIL_PATTERNS = ((128, 1), (512, 4), (2048, 16))
DIL_HALF = 64
RMS_EPS = 1e-6
NEG = -1e30

LANES = 128
HEAD_PAIRS = NA_HEADS // 2
QKV_TILE = 512
TAIL_TILE = 256
DIL_QBLK = 128
DIL_KWIN = 256
DIL_STEP_TOKENS = 2048
FFN_CHUNK = 256
VMEM_LIMIT = 56 * 1024 * 1024

assert all(w // (2 * d) == DIL_HALF for w, d in DIL_PATTERNS)


def _rms(x, g):
    return x * lax.rsqrt(jnp.mean(x * x, axis=-1, keepdims=True) + RMS_EPS) * g


def _dot_nt(a, b):
    return lax.dot_general(a, b, (((1,), (1,)), ((), ())), preferred_element_type=jnp.float32)


def _dot(a, b):
    return jnp.dot(a, b, preferred_element_type=jnp.float32)


def _qkv_kernel(x_ref, g_ref, w_ref, nat_ref, d4_ref, d16_ref, slab_ref):
    tm = x_ref.shape[1]
    hn = _rms(x_ref[0], g_ref[...]).astype(jnp.bfloat16)
    chunk = NA_WIDTH
    for c in range(QKV_WIDTH // chunk):
        acc = _dot(hn, w_ref[:, c * chunk:(c + 1) * chunk])
        if c % 3 == 0:
            acc = acc * (HEAD_DIM ** -0.5)
        nat_ref[0, :, c * chunk:(c + 1) * chunk] = acc.astype(jnp.bfloat16)
        if c >= 3:
            for j in range(chunk // LANES):
                slab_ref[(c - 3) * (chunk // LANES) + j] = acc[:, j * LANES:(j + 1) * LANES]
    for d, out_ref in ((4, d4_ref), (16, d16_ref)):
        rows = tm // d
        for p in range(d):
            for s in range(DIL_QKV // LANES):
                out_ref[0, p, :, s * LANES:(s + 1) * LANES] = (
                    slab_ref[s, pl.ds(p, rows, stride=d), :].astype(jnp.bfloat16))


def _qkv_call(x, g, w):
    b, t, _ = x.shape
    tm = QKV_TILE
    return pl.pallas_call(
        _qkv_kernel,
        grid=(b, t // tm),
        in_specs=[
            pl.BlockSpec((1, tm, D_MODEL), lambda i, j: (i, j, 0)),
            pl.BlockSpec((1, D_MODEL), lambda i, j: (0, 0)),
            pl.BlockSpec((D_MODEL, QKV_WIDTH), lambda i, j: (0, 0), pipeline_mode=pl.Buffered(1)),
        ],
        out_specs=[
            pl.BlockSpec((1, tm, QKV_WIDTH), lambda i, j: (i, j, 0)),
            pl.BlockSpec((1, 4, tm // 4, DIL_QKV), lambda i, j: (i, 0, j, 0)),
            pl.BlockSpec((1, 16, tm // 16, DIL_QKV), lambda i, j: (i, 0, j, 0)),
        ],
        out_shape=[
            jax.ShapeDtypeStruct((b, t, QKV_WIDTH), jnp.bfloat16),
            jax.ShapeDtypeStruct((b, 4, t // 4, DIL_QKV), jnp.bfloat16),
            jax.ShapeDtypeStruct((b, 16, t // 16, DIL_QKV), jnp.bfloat16),
        ],
        scratch_shapes=[pltpu.VMEM((DIL_QKV // LANES, tm, LANES), jnp.float32)],
        compiler_params=pltpu.CompilerParams(
            dimension_semantics=("parallel", "parallel"), vmem_limit_bytes=VMEM_LIMIT),
        name="qkv_proj",
    )(x, g, w)


def _pair_attention(qb, kb, vb, bias0, bias1):
    lane = lax.broadcasted_iota(jnp.int32, qb.shape, 1)
    outs = []
    for h, bias in enumerate((bias0, bias1)):
        mine = (lane < HEAD_DIM) if h == 0 else (lane >= HEAD_DIM)
        qh = jnp.where(mine, qb, jnp.zeros_like(qb))
        s = _dot_nt(qh, kb) + bias
        mx = jnp.max(s, axis=-1, keepdims=True)
        e = jnp.exp(s - mx)
        den = jnp.sum(e, axis=-1, keepdims=True)
        pv = _dot(e.astype(jnp.bfloat16), vb)
        outs.append((pv / den, mx, den))
    return outs


def _na_kernel(q_ref, k_ref, v_ref, tab_ref, o_ref, *, rows):
    lane = lax.broadcasted_iota(jnp.int32, (GRID_W, LANES), 1)

    def body(r, carry):
        rs = jnp.clip(r - NA_ROWS // 2, 0, rows - NA_ROWS)
        var = r - rs
        q0 = pl.multiple_of(r * GRID_W, GRID_W)
        k0 = pl.multiple_of(rs * GRID_W, GRID_W)
        qb = q_ref[0, pl.ds(q0, GRID_W), :]
        kb = k_ref[0, pl.ds(k0, NA_ROWS * GRID_W), :]
        vb = v_ref[0, pl.ds(k0, NA_ROWS * GRID_W), :]
        (o0, _, _), (o1, _, _) = _pair_attention(qb, kb, vb, tab_ref[0, var], tab_ref[1, var])
        o_ref[0, pl.ds(q0, GRID_W), :] = jnp.where(lane < HEAD_DIM, o0, o1)
        return carry

    lax.fori_loop(0, rows, body, 0)


def _na_call(nat, tab):
    b, t, _ = nat.shape
    rows = t // GRID_W
    kw = NA_ROWS * GRID_W
    col = lambda off: pl.BlockSpec((1, t, LANES), lambda i, j, off=off: (i, 0, off + j))
    return pl.pallas_call(
        functools.partial(_na_kernel, rows=rows),
        grid=(b, HEAD_PAIRS),
        in_specs=[
            col(0), col(HEAD_PAIRS), col(2 * HEAD_PAIRS),
            pl.BlockSpec((2, NA_ROWS, GRID_W, kw), lambda i, j: (j, 0, 0, 0)),
        ],
        out_specs=pl.BlockSpec((1, t, LANES), lambda i, j: (i, 0, j)),
        out_shape=jax.ShapeDtypeStruct((b, t, NA_WIDTH), jnp.float32),
        compiler_params=pltpu.CompilerParams(
            dimension_semantics=("parallel", "parallel"), vmem_limit_bytes=VMEM_LIMIT),
        name="na_attn",
    )(nat, nat, nat, tab)


def _dil_kernel(q_ref, k_ref, v_ref, tab_ref, o_ref, lse_ref, *, seq, group, kwin):
    nblk = seq // DIL_QBLK
    lane = lax.broadcasted_iota(jnp.int32, (DIL_QBLK, LANES), 1)

    def body(i, carry):
        g = lax.div(i, nblk)
        m = lax.rem(i, nblk)
        q0 = pl.multiple_of(m * DIL_QBLK, DIL_QBLK)
        k0 = pl.multiple_of(jnp.clip(q0 - DIL_HALF, 0, seq - kwin), DIL_HALF)
        var = lax.div(q0 - k0, DIL_HALF)
        qb = q_ref[g, pl.ds(q0, DIL_QBLK), :]
        kb = k_ref[g, pl.ds(k0, kwin), :]
        vb = v_ref[g, pl.ds(k0, kwin), :]
        (o0, m0, d0), (o1, m1, d1) = _pair_attention(
            qb, kb, vb, tab_ref[var, 0], tab_ref[var, 1])
        o_ref[g, pl.ds(q0, DIL_QBLK), :] = jnp.where(lane < HEAD_DIM, o0, o1)
        lse_ref[g, pl.ds(q0, DIL_QBLK), :] = jnp.where(
            lane < HEAD_DIM, m0 + jnp.log(d0), m1 + jnp.log(d1))
        return carry

    lax.fori_loop(0, group * nblk, body, 0)


def _dil_tables(dilation, kwin):
    slopes = np.array([2.0 ** (-8.0 * (i + 1) / DIL_HEADS) for i in range(DIL_HEADS)], np.float32)
    i = np.arange(DIL_QBLK)[:, None]
    j = np.arange(kwin)[None, :]
    tabs = []
    for v in range(3):
        diff = np.abs(j - i - v * DIL_HALF)
        dist = (dilation * diff).astype(np.float32)
        bias = -slopes[:, None, None] * dist[None]
        tabs.append(np.where((diff <= DIL_HALF)[None], bias, np.float32(NEG)))
    return jnp.asarray(np.stack(tabs).astype(np.float32))


def _dil_call(arr, col0, dilation):
    n, seq, _ = arr.shape
    kwin = min(DIL_KWIN, seq)
    group = max(1, min(n, DIL_STEP_TOKENS // seq))
    assert n % group == 0 and seq % DIL_QBLK == 0
    col = lambda off: pl.BlockSpec((group, seq, LANES), lambda i, j, off=off: (i, 0, off + j))
    out_spec = pl.BlockSpec((group, seq, LANES), lambda i, j: (i, 0, j))
    out_sds = jax.ShapeDtypeStruct((n, seq, DIL_WIDTH), jnp.float32)
    return pl.pallas_call(
        functools.partial(_dil_kernel, seq=seq, group=group, kwin=kwin),
        grid=(n // group, HEAD_PAIRS),
        in_specs=[
            col(col0), col(col0 + HEAD_PAIRS), col(col0 + 2 * HEAD_PAIRS),
            pl.BlockSpec((3, 2, DIL_QBLK, kwin), lambda i, j: (0, j, 0, 0)),
        ],
        out_specs=[out_spec, out_spec],
        out_shape=[out_sds, out_sds],
        compiler_params=pltpu.CompilerParams(
            dimension_semantics=("parallel", "parallel"), vmem_limit_bytes=VMEM_LIMIT),
        name=f"dil_attn_d{dilation}",
    )(arr, arr, arr, _dil_tables(dilation, kwin))


def _tail_kernel(x_ref, oa_ref, o1_ref, l1_ref, o4_ref, l4_ref, o16_ref, l16_ref,
                 gna_ref, gdil_ref, wout_ref, gffn_ref, wg_ref, wu_ref, wd_ref, gfin_ref,
                 y_ref, o4s, l4s, o16s, l16s):
    tm = x_ref.shape[1]
    nslab = DIL_WIDTH // LANES
    for d, src, dst in ((4, o4_ref, o4s), (4, l4_ref, l4s), (16, o16_ref, o16s), (16, l16_ref, l16s)):
        rows = tm // d
        for p in range(d):
            for s in range(nslab):
                dst[s, pl.ds(p, rows, stride=d), :] = src[0, p, :, s * LANES:(s + 1) * LANES]

    od_parts = []
    for s in range(nslab):
        sl = slice(s * LANES, (s + 1) * LANES)
        l1, l4, l16 = l1_ref[0, :, sl], l4s[s], l16s[s]
        top = jnp.maximum(jnp.maximum(l1, l4), l16)
        w1, w4, w16 = jnp.exp(l1 - top), jnp.exp(l4 - top), jnp.exp(l16 - top)
        num = w1 * o1_ref[0, :, sl] + w4 * o4s[s] + w16 * o16s[s]
        od_parts.append(num / (w1 + w4 + w16))
    od = jnp.concatenate(od_parts, axis=-1)

    na = _rms(oa_ref[0], gna_ref[...]).astype(jnp.bfloat16)
    dn = _rms(od, gdil_ref[...]).astype(jnp.bfloat16)
    x2 = x_ref[0] + _dot(na, wout_ref[:NA_WIDTH, :]) + _dot(dn, wout_ref[NA_WIDTH:, :])

    hn = _rms(x2, gffn_ref[...]).astype(jnp.bfloat16)
    acc = jnp.zeros((tm, D_MODEL), jnp.float32)
    for c in range(D_FF // FFN_CHUNK):
        sl = slice(c * FFN_CHUNK, (c + 1) * FFN_CHUNK)
        gate = _dot(hn, wg_ref[:, sl])
        up = _dot(hn, wu_ref[:, sl])
        act = (gate * jax.nn.sigmoid(gate) * up).astype(jnp.bfloat16)
        acc = acc + _dot(act, wd_ref[sl, :])
    y_ref[0] = _rms(x2 + acc, gfin_ref[...])


def _tail_call(x, oa, o1, l1, o4, l4, o16, l16, g_na, g_dil, w_out, g_ffn, w_gate, w_up, w_down,
               g_final):
    b, t, _ = x.shape
    tm = TAIL_TILE
    tok = lambda width: pl.BlockSpec((1, tm, width), lambda i, j: (i, j, 0))
    plane = lambda d: pl.BlockSpec((1, d, tm // d, DIL_WIDTH), lambda i, j: (i, 0, j, 0))
    const = lambda shape: pl.BlockSpec(shape, lambda i, j: (0,) * len(shape),
                                       pipeline_mode=pl.Buffered(1))
    slab = pltpu.VMEM((DIL_WIDTH // LANES, tm, LANES), jnp.float32)
    return pl.pallas_call(
        _tail_kernel,
        grid=(b, t // tm),
        in_specs=[
            tok(D_MODEL), tok(NA_WIDTH), tok(DIL_WIDTH), tok(DIL_WIDTH),
            plane(4), plane(4), plane(16), plane(16),
            const((1, NA_WIDTH)), const((1, DIL_WIDTH)), const((D_MODEL, D_MODEL)),
            const((1, D_MODEL)), const((D_MODEL, D_FF)), const((D_MODEL, D_FF)),
            const((D_FF, D_MODEL)), const((1, D_MODEL)),
        ],
        out_specs=tok(D_MODEL),
        out_shape=jax.ShapeDtypeStruct((b, t, D_MODEL), jnp.float32),
        scratch_shapes=[slab, slab, slab, slab],
        compiler_params=pltpu.CompilerParams(
            dimension_semantics=("parallel", "parallel"), vmem_limit_bytes=VMEM_LIMIT),
        name="tail",
    )(x, oa, o1, l1, o4, l4, o16, l16, g_na, g_dil, w_out, g_ffn, w_gate, w_up, w_down, g_final)


def _na_table(rpb):
    a = np.arange(NA_ROWS)[:, None, None, None]
    qc = np.arange(GRID_W)[None, :, None, None]
    ki = np.arange(NA_ROWS)[None, None, :, None]
    kc = np.arange(GRID_W)[None, None, None, :]
    wstart = np.clip(qc - NA_COLS // 2, 0, GRID_W - NA_COLS)
    ok = (kc >= wstart) & (kc < wstart + NA_COLS)
    roff = np.broadcast_to(ki - a + NA_ROWS - 1, (NA_ROWS, GRID_W, NA_ROWS, GRID_W))
    relc = np.broadcast_to(np.clip(kc - qc + NA_COLS - 1, 0, 2 * NA_COLS - 2), roff.shape)
    ok = np.broadcast_to(ok, roff.shape)
    shape = (NA_ROWS, GRID_W, NA_ROWS * GRID_W)
    bias = rpb.astype(jnp.float32)[:, roff.reshape(shape), relc.reshape(shape)]
    return jnp.where(jnp.asarray(ok.reshape(shape))[None], bias, NEG)


def _trunk(x, w_in, rpb, g_attn, g_na, g_dil, w_out, g_ffn, w_gate, w_up, w_down, g_final):
    b, t, _ = x.shape
    nat, d4, d16 = _qkv_call(x, g_attn, w_in)
    oa = _na_call(nat, _na_table(rpb))
    o1, l1 = _dil_call(nat, (NA_WIDTH * 3) // LANES, 1)
    o4, l4 = _dil_call(d4.reshape(b * 4, t // 4, DIL_QKV), 0, 4)
    o16, l16 = _dil_call(d16.reshape(b * 16, t // 16, DIL_QKV), 0, 16)
    plane = lambda z, d: z.reshape(b, d, t // d, DIL_WIDTH)
    return _tail_call(x, oa, o1, l1, plane(o4, 4), plane(l4, 4), plane(o16, 16), plane(l16, 16),
                      g_na, g_dil, w_out, g_ffn, w_gate, w_up, w_down, g_final)


def kernel(x_prompt, x_sample, w_in, rpb, g_attn, g_na, g_dil, w_out, g_ffn, w_gate, w_up, w_down,
           g_final):
    assert w_in.shape[0] == 1, "single-layer trunk"
    bf = lambda w: w[0].astype(jnp.bfloat16)
    params = (bf(w_in), rpb[0], g_attn, g_na, g_dil, bf(w_out), g_ffn, bf(w_gate), bf(w_up),
              bf(w_down), g_final.reshape(1, D_MODEL))
    return _trunk(x_prompt, *params), _trunk(x_sample, *params)
```

```python
import functools

import jax
import jax.numpy as jnp
import numpy as np
from jax import lax
from jax.experimental import pallas as pl
from jax.experimental.pallas import tpu as pltpu

D_MODEL = 1024
HEAD_DIM = 64
NA_HEADS = 8
DIL_HEADS = 8
NA_WIDTH = NA_HEADS * HEAD_DIM
DIL_WIDTH = DIL_HEADS * HEAD_DIM
QKV_WIDTH = 3 * (NA_WIDTH + DIL_WIDTH)
DIL_QKV = 3 * DIL_WIDTH
D_FF = 2816
GRID_W = 64
NA_ROWS = 8
NA_COLS = 16
DIL_PATTERNS = ((128, 1), (512, 4), (2048, 16))
DIL_HALF = 64
RMS_EPS = 1e-6
NEG = -1e30

LANES = 128
HEAD_PAIRS = NA_HEADS // 2
QKV_TILE = 512
TAIL_TILE = 512
DIL_QBLK = 128
DIL_KWIN = 256
DIL_STEP_TOKENS = 2048
FFN_CHUNK = 256
NA_STEP_ROWS = 16
ATTN_UNROLL = 8
VMEM_LIMIT = 56 * 1024 * 1024

assert all(w // (2 * d) == DIL_HALF for w, d in DIL_PATTERNS)


def _rms(x, g):
    return x * lax.rsqrt(jnp.mean(x * x, axis=-1, keepdims=True) + RMS_EPS) * g


def _dot_nt(a, b):
    return lax.dot_general(a, b, (((1,), (1,)), ((), ())), preferred_element_type=jnp.float32)


def _dot(a, b):
    return jnp.dot(a, b, preferred_element_type=jnp.float32)


def _qkv_kernel(x_ref, g_ref, w_ref, nat_ref, d4_ref, d16_ref, slab_ref):
    tm = x_ref.shape[1]
    hn = _rms(x_ref[0], g_ref[...]).astype(jnp.bfloat16)
    chunk = NA_WIDTH
    for c in range(QKV_WIDTH // chunk):
        acc = _dot(hn, w_ref[:, c * chunk:(c + 1) * chunk])
        if c % 3 == 0:
            acc = acc * (HEAD_DIM ** -0.5)
        nat_ref[0, :, c * chunk:(c + 1) * chunk] = acc.astype(jnp.bfloat16)
        if c >= 3:
            for j in range(chunk // LANES):
                slab_ref[(c - 3) * (chunk // LANES) + j] = acc[:, j * LANES:(j + 1) * LANES]
    for d, out_ref in ((4, d4_ref), (16, d16_ref)):
        rows = tm // d
        for p in range(d):
            for s in range(DIL_QKV // LANES):
                out_ref[0, p, :, s * LANES:(s + 1) * LANES] = (
                    slab_ref[s, pl.ds(p, rows, stride=d), :].astype(jnp.bfloat16))


def _qkv_call(x, g, w):
    b, t, _ = x.shape
    tm = QKV_TILE
    return pl.pallas_call(
        _qkv_kernel,
        grid=(b, t // tm),
        in_specs=[
            pl.BlockSpec((1, tm, D_MODEL), lambda i, j: (i, j, 0)),
            pl.BlockSpec((1, D_MODEL), lambda i, j: (0, 0)),
            pl.BlockSpec((D_MODEL, QKV_WIDTH), lambda i, j: (0, 0), pipeline_mode=pl.Buffered(1)),
        ],
        out_specs=[
            pl.BlockSpec((1, tm, QKV_WIDTH), lambda i, j: (i, j, 0)),
            pl.BlockSpec((1, 4, tm // 4, DIL_QKV), lambda i, j: (i, 0, j, 0)),
            pl.BlockSpec((1, 16, tm // 16, DIL_QKV), lambda i, j: (i, 0, j, 0)),
        ],
        out_shape=[
            jax.ShapeDtypeStruct((b, t, QKV_WIDTH), jnp.bfloat16),
            jax.ShapeDtypeStruct((b, 4, t // 4, DIL_QKV), jnp.bfloat16),
            jax.ShapeDtypeStruct((b, 16, t // 16, DIL_QKV), jnp.bfloat16),
        ],
        scratch_shapes=[pltpu.VMEM((DIL_QKV // LANES, tm, LANES), jnp.float32)],
        compiler_params=pltpu.CompilerParams(
            dimension_semantics=("parallel", "parallel"), vmem_limit_bytes=VMEM_LIMIT),
        name="qkv_proj",
    )(x, g, w)


def _low_lanes(shape):
    return lax.broadcasted_iota(jnp.int32, shape, len(shape) - 1) < HEAD_DIM


def _fill_value_planes(v, vaug_ref):
    low = _low_lanes(v.shape)
    one = jnp.ones_like(v)
    vaug_ref[0] = jnp.where(low, v, one)
    vaug_ref[1] = jnp.where(low, one, v)


def _staged_attention(nblk, geom, load_q, load_k, load_v, bias, emit, s_ref, mx_ref):
    m = s_ref.shape[1]
    kwin = s_ref.shape[2]
    low = _low_lanes((m, LANES))

    def scores(i, carry):
        qidx, kidx, var = geom(i)
        qb = load_q(qidx)
        zero = jnp.zeros_like(qb)
        q2 = jnp.concatenate([jnp.where(low, qb, zero), jnp.where(low, zero, qb)], axis=0)
        s2 = _dot_nt(q2, load_k(kidx))
        for h in range(2):
            s = s2[h * m:(h + 1) * m] + bias(var, h)
            s_ref[2 * i + h] = s
            mx_ref[2 * i + h] = jnp.broadcast_to(jnp.max(s, axis=-1, keepdims=True), (m, LANES))
        return carry

    def values(i, carry):
        qidx, kidx, _ = geom(i)
        pv = []
        for h in range(2):
            u = 2 * i + h
            mx = mx_ref[u]
            e = jnp.concatenate(
                [jnp.exp(s_ref[u, :, c * LANES:(c + 1) * LANES] - mx) for c in range(kwin // LANES)],
                axis=1)
            pv.append(_dot(e.astype(jnp.bfloat16), load_v(h, kidx)))
        num = jnp.where(low, pv[0], pv[1])
        den = pltpu.roll(jnp.where(low, pv[1], pv[0]), HEAD_DIM, 1)
        emit(qidx, num / den, jnp.where(low, mx_ref[2 * i], mx_ref[2 * i + 1]), den)
        return carry

    nbatch = nblk // ATTN_UNROLL
    for b in range(nbatch + 1):
        for i in range(ATTN_UNROLL):
            if b < nbatch:
                scores(b * ATTN_UNROLL + i, 0)
            if b >= 1:
                values((b - 1) * ATTN_UNROLL + i, 0)


def _attn_scratch(nblk, m, kwin, vaug_shape):
    return [
        pltpu.VMEM((2 * nblk, m, kwin), jnp.float32),
        pltpu.VMEM((2 * nblk, m, LANES), jnp.float32),
        pltpu.VMEM((2,) + vaug_shape, jnp.bfloat16),
    ]


def _na_kernel(q_ref, k_ref, v_ref, tab_ref, o_ref, s_ref, mx_ref, vaug_ref, *, rows):
    step = pl.program_id(2)

    @pl.when(step == 0)
    def _():
        _fill_value_planes(v_ref[0], vaug_ref)

    def geom(i):
        r = step * NA_STEP_ROWS + i
        rs = jnp.clip(r - NA_ROWS // 2, 0, rows - NA_ROWS)
        return (pl.multiple_of(i * GRID_W, GRID_W), pl.multiple_of(rs * GRID_W, GRID_W), r - rs)

    q0 = pl.multiple_of(step * (NA_STEP_ROWS * GRID_W), NA_STEP_ROWS * GRID_W)

    def emit(qidx, out, mx, den):
        o_ref[0, pl.ds(qidx, GRID_W), :] = out

    _staged_attention(
        NA_STEP_ROWS, geom,
        load_q=lambda qidx: q_ref[0, pl.ds(q0 + qidx, GRID_W), :],
        load_k=lambda kidx: k_ref[0, pl.ds(kidx, NA_ROWS * GRID_W), :],
        load_v=lambda h, kidx: vaug_ref[h, pl.ds(kidx, NA_ROWS * GRID_W), :],
        bias=lambda var, h: tab_ref[h, var],
        emit=emit, s_ref=s_ref, mx_ref=mx_ref)


def _na_call(nat, tab):
    b, t, _ = nat.shape
    rows = t // GRID_W
    kw = NA_ROWS * GRID_W
    col = lambda off: pl.BlockSpec((1, t, LANES), lambda i, j, s, off=off: (i, 0, off + j))
    return pl.pallas_call(
        functools.partial(_na_kernel, rows=rows),
        grid=(b, HEAD_PAIRS, rows // NA_STEP_ROWS),
        in_specs=[
            col(0), col(HEAD_PAIRS), col(2 * HEAD_PAIRS),
            pl.BlockSpec((2, NA_ROWS, GRID_W, kw), lambda i, j, s: (j, 0, 0, 0)),
        ],
        out_specs=pl.BlockSpec((1, NA_STEP_ROWS * GRID_W, LANES), lambda i, j, s: (i, s, j)),
        out_shape=jax.ShapeDtypeStruct((b, t, NA_WIDTH), jnp.float32),
        scratch_shapes=_attn_scratch(NA_STEP_ROWS, GRID_W, kw, (t, LANES)),
        compiler_params=pltpu.CompilerParams(
            dimension_semantics=("parallel", "parallel", "arbitrary"),
            vmem_limit_bytes=VMEM_LIMIT),
        name="na_attn",
    )(nat, nat, nat, tab)


def _dil_kernel(q_ref, k_ref, v_ref, tab_ref, o_ref, lse_ref, s_ref, mx_ref, vaug_ref,
                *, seq, group, kwin):
    nblk = seq // DIL_QBLK
    _fill_value_planes(v_ref[...], vaug_ref)

    def geom(i):
        g, q0 = i // nblk, (i % nblk) * DIL_QBLK
        k0 = min(max(q0 - DIL_HALF, 0), seq - kwin)
        return (g, q0), (g, k0), (q0 - k0) // DIL_HALF

    def emit(qidx, out, mx, den):
        g, q0 = qidx
        o_ref[g, pl.ds(q0, DIL_QBLK), :] = out
        lse_ref[g, pl.ds(q0, DIL_QBLK), :] = mx + jnp.log(den)

    _staged_attention(
        group * nblk, geom,
        load_q=lambda qidx: q_ref[qidx[0], pl.ds(qidx[1], DIL_QBLK), :],
        load_k=lambda kidx: k_ref[kidx[0], pl.ds(kidx[1], kwin), :],
        load_v=lambda h, kidx: vaug_ref[h, kidx[0], pl.ds(kidx[1], kwin), :],
        bias=lambda var, h: tab_ref[var, h],
        emit=emit, s_ref=s_ref, mx_ref=mx_ref)


def _dil_tables(dilation, kwin):
    slopes = np.array([2.0 ** (-8.0 * (i + 1) / DIL_HEADS) for i in range(DIL_HEADS)], np.float32)
    i = np.arange(DIL_QBLK)[:, None]
    j = np.arange(kwin)[None, :]
    tabs = []
    for v in range(3):
        diff = np.abs(j - i - v * DIL_HALF)
        dist = (dilation * diff).astype(np.float32)
        bias = -slopes[:, None, None] * dist[None]
        tabs.append(np.where((diff <= DIL_HALF)[None], bias, np.float32(NEG)))
    return jnp.asarray(np.stack(tabs).astype(np.float32))


def _dil_call(arr, col0, dilation):
    n, seq, _ = arr.shape
    kwin = min(DIL_KWIN, seq)
    group = max(1, min(n, DIL_STEP_TOKENS // seq))
    assert n % group == 0 and seq % DIL_QBLK == 0
    col = lambda off: pl.BlockSpec((group, seq, LANES), lambda i, j, off=off: (i, 0, off + j))
    out_spec = pl.BlockSpec((group, seq, LANES), lambda i, j: (i, 0, j))
    out_sds = jax.ShapeDtypeStruct((n, seq, DIL_WIDTH), jnp.float32)
    return pl.pallas_call(
        functools.partial(_dil_kernel, seq=seq, group=group, kwin=kwin),
        grid=(n // group, HEAD_PAIRS),
        in_specs=[
            col(col0), col(col0 + HEAD_PAIRS), col(col0 + 2 * HEAD_PAIRS),
            pl.BlockSpec((3, 2, DIL_QBLK, kwin), lambda i, j: (0, j, 0, 0)),
        ],
        out_specs=[out_spec, out_spec],
        out_shape=[out_sds, out_sds],
        scratch_shapes=_attn_scratch(group * seq // DIL_QBLK, DIL_QBLK, kwin, (group, seq, LANES)),
        compiler_params=pltpu.CompilerParams(
            dimension_semantics=("parallel", "parallel"), vmem_limit_bytes=VMEM_LIMIT),
        name=f"dil_attn_d{dilation}",
    )(arr, arr, arr, _dil_tables(dilation, kwin))


def _tail_kernel(x_ref, oa_ref, o1_ref, l1_ref, o4_ref, l4_ref, o16_ref, l16_ref,
                 gna_ref, gdil_ref, wout_ref, gffn_ref, wg_ref, wu_ref, wd_ref, gfin_ref,
                 y_ref, o4s, l4s, o16s, l16s, act_ref):
    tm = x_ref.shape[1]
    nslab = DIL_WIDTH // LANES
    for d, src, dst in ((4, o4_ref, o4s), (4, l4_ref, l4s), (16, o16_ref, o16s), (16, l16_ref, l16s)):
        rows = tm // d
        for p in range(d):
            for s in range(nslab):
                dst[s, pl.ds(p, rows, stride=d), :] = src[0, p, :, s * LANES:(s + 1) * LANES]

    od_parts = []
    for s in range(nslab):
        sl = slice(s * LANES, (s + 1) * LANES)
        l1, l4, l16 = l1_ref[0, :, sl], l4s[s], l16s[s]
        top = jnp.maximum(jnp.maximum(l1, l4), l16)
        w1, w4, w16 = jnp.exp(l1 - top), jnp.exp(l4 - top), jnp.exp(l16 - top)
        num = w1 * o1_ref[0, :, sl] + w4 * o4s[s] + w16 * o16s[s]
        od_parts.append(num / (w1 + w4 + w16))
    od = jnp.concatenate(od_parts, axis=-1)

    na = _rms(oa_ref[0], gna_ref[...]).astype(jnp.bfloat16)
    dn = _rms(od, gdil_ref[...]).astype(jnp.bfloat16)
    x2 = x_ref[0] + _dot(na, wout_ref[:NA_WIDTH, :]) + _dot(dn, wout_ref[NA_WIDTH:, :])

    hn = _rms(x2, gffn_ref[...]).astype(jnp.bfloat16)
    for c in range(D_FF // FFN_CHUNK):
        sl = slice(c * FFN_CHUNK, (c + 1) * FFN_CHUNK)
        gate = _dot(hn, wg_ref[:, sl])
        up = _dot(hn, wu_ref[:, sl])
        act_ref[:, sl] = (gate * jax.nn.sigmoid(gate) * up).astype(jnp.bfloat16)
    y_ref[0] = _rms(x2 + _dot(act_ref[...], wd_ref[...]), gfin_ref[...])


def _tail_call(x, oa, o1, l1, o4, l4, o16, l16, g_na, g_dil, w_out, g_ffn, w_gate, w_up, w_down,
               g_final):
    b, t, _ = x.shape
    tm = TAIL_TILE
    tok = lambda width: pl.BlockSpec((1, tm, width), lambda i, j: (i, j, 0))
    plane = lambda d: pl.BlockSpec((1, d, tm // d, DIL_WIDTH), lambda i, j: (i, 0, j, 0))
    const = lambda shape: pl.BlockSpec(shape, lambda i, j: (0,) * len(shape),
                                       pipeline_mode=pl.Buffered(1))
    slab = pltpu.VMEM((DIL_WIDTH // LANES, tm, LANES), jnp.float32)
    return pl.pallas_call(
        _tail_kernel,
        grid=(b, t // tm),
        in_specs=[
            tok(D_MODEL), tok(NA_WIDTH), tok(DIL_WIDTH), tok(DIL_WIDTH),
            plane(4), plane(4), plane(16), plane(16),
            const((1, NA_WIDTH)), const((1, DIL_WIDTH)), const((D_MODEL, D_MODEL)),
            const((1, D_MODEL)), const((D_MODEL, D_FF)), const((D_MODEL, D_FF)),
            const((D_FF, D_MODEL)), const((1, D_MODEL)),
        ],
        out_specs=tok(D_MODEL),
        out_shape=jax.ShapeDtypeStruct((b, t, D_MODEL), jnp.float32),
        scratch_shapes=[slab, slab, slab, slab, pltpu.VMEM((tm, D_FF), jnp.bfloat16)],
        compiler_params=pltpu.CompilerParams(
            dimension_semantics=("parallel", "parallel"), vmem_limit_bytes=VMEM_LIMIT),
        name="tail",
    )(x, oa, o1, l1, o4, l4, o16, l16, g_na, g_dil, w_out, g_ffn, w_gate, w_up, w_down, g_final)


def _na_table(rpb):
    qc = np.arange(GRID_W)[:, None]
    kc = np.arange(GRID_W)[None, :]
    wstart = np.clip(qc - NA_COLS // 2, 0, GRID_W - NA_COLS)
    ok = (kc >= wstart) & (kc < wstart + NA_COLS)
    relc = np.clip(kc - qc + NA_COLS - 1, 0, 2 * NA_COLS - 2)
    onehot = (relc[None] == np.arange(2 * NA_COLS - 1)[:, None, None]).astype(np.float32)
    colbias = jnp.einsum("hrc,cqk->hrqk", rpb.astype(jnp.float32), jnp.asarray(onehot),
                         precision=lax.Precision.HIGHEST)
    colbias = jnp.where(jnp.asarray(ok)[None, None], colbias, NEG)
    variants = []
    for a in range(NA_ROWS):
        win = colbias[:, NA_ROWS - 1 - a:2 * NA_ROWS - 1 - a]
        variants.append(win.transpose(0, 2, 1, 3).reshape(NA_HEADS, GRID_W, NA_ROWS * GRID_W))
    return jnp.stack(variants, axis=1)


def _trunk(x, w_in, na_tab, g_attn, g_na, g_dil, w_out, g_ffn, w_gate, w_up, w_down, g_final):
    b, t, _ = x.shape
    nat, d4, d16 = _qkv_call(x, g_attn, w_in)
    oa = _na_call(nat, na_tab)
    o1, l1 = _dil_call(nat, (NA_WIDTH * 3) // LANES, 1)
    o4, l4 = _dil_call(d4.reshape(b * 4, t // 4, DIL_QKV), 0, 4)
    o16, l16 = _dil_call(d16.reshape(b * 16, t // 16, DIL_QKV), 0, 16)
    plane = lambda z, d: z.reshape(b, d, t // d, DIL_WIDTH)
    return _tail_call(x, oa, o1, l1, plane(o4, 4), plane(l4, 4), plane(o16, 16), plane(l16, 16),
                      g_na, g_dil, w_out, g_ffn, w_gate, w_up, w_down, g_final)


def kernel(x_prompt, x_sample, w_in, rpb, g_attn, g_na, g_dil, w_out, g_ffn, w_gate, w_up, w_down,
           g_final):
    assert w_in.shape[0] == 1, "single-layer trunk"
    bf = lambda w: w[0].astype(jnp.bfloat16)
    params = (bf(w_in), _na_table(rpb[0]), g_attn, g_na, g_dil, bf(w_out), g_ffn, bf(w_gate), bf(w_up),
              bf(w_down), g_final.reshape(1, D_MODEL))
    return _trunk(x_prompt, *params), _trunk(x_sample, *params)
```

```python
import functools

import jax
import jax.numpy as jnp
import numpy as np
from jax import lax
from jax.experimental import pallas as pl
from jax.experimental.pallas import tpu as pltpu

D_MODEL = 1024
HEAD_DIM = 64
NA_HEADS = 8
DIL_HEADS = 8
NA_WIDTH = NA_HEADS * HEAD_DIM
DIL_WIDTH = DIL_HEADS * HEAD_DIM
QKV_WIDTH = 3 * (NA_WIDTH + DIL_WIDTH)
DIL_QKV = 3 * DIL_WIDTH
D_FF = 2816
GRID_W = 64
NA_ROWS = 8
NA_COLS = 16
DIL_PATTERNS = ((128, 1), (512, 4), (2048, 16))
DIL_HALF = 64
RMS_EPS = 1e-6
NEG = -1e30

LANES = 128
HEAD_PAIRS = NA_HEADS // 2
QKV_TILE = 512
TAIL_TILE = 512
DIL_QBLK = 128
DIL_KWIN = 256
DIL_STEP_TOKENS = 4096
FFN_CHUNK = 256
QKV_CHUNK = 256
NA_STEP_ROWS = 32
NA_BATCH = 4
DIL_BATCH = 4
VMEM_LIMIT = 56 * 1024 * 1024

assert all(w // (2 * d) == DIL_HALF for w, d in DIL_PATTERNS)


def _rms(x, g):
    return x * lax.rsqrt(jnp.mean(x * x, axis=-1, keepdims=True) + RMS_EPS) * g


def _dot_nt(a, b):
    return lax.dot_general(a, b, (((1,), (1,)), ((), ())), preferred_element_type=jnp.float32)


def _dot(a, b):
    return jnp.dot(a, b, preferred_element_type=jnp.float32)


def _qkv_kernel(x_ref, g_ref, w_ref, nat_ref, d4_ref, d16_ref, slab_ref, hn_ref):
    tm = x_ref.shape[1]
    hn_ref[...] = _rms(x_ref[0], g_ref[...]).astype(jnp.bfloat16)
    chunk = QKV_CHUNK
    slabs = chunk // LANES
    nchunk = QKV_WIDTH // chunk
    per_role = NA_WIDTH // chunk
    for c in list(range(nchunk // 2, nchunk)) + list(range(nchunk // 2)):
        acc = _dot(hn_ref[...], w_ref[:, c * chunk:(c + 1) * chunk])
        if (c // per_role) % 3 == 0:
            acc = acc * (HEAD_DIM ** -0.5)
        nat_ref[0, :, c * chunk:(c + 1) * chunk] = acc.astype(jnp.bfloat16)
        if c < nchunk // 2:
            continue
        s0 = (c - nchunk // 2) * slabs
        for j in range(slabs):
            slab_ref[s0 + j] = acc[:, j * LANES:(j + 1) * LANES]
        for d, out_ref in ((4, d4_ref), (16, d16_ref)):
            rows = tm // d
            for p in range(d):
                for s in range(s0, s0 + slabs):
                    out_ref[0, p, :, s * LANES:(s + 1) * LANES] = (
                        slab_ref[s, pl.ds(p, rows, stride=d), :].astype(jnp.bfloat16))


def _qkv_call(x, g, w):
    b, t, _ = x.shape
    tm = QKV_TILE
    return pl.pallas_call(
        _qkv_kernel,
        grid=(b, t // tm),
        in_specs=[
            pl.BlockSpec((1, tm, D_MODEL), lambda i, j: (i, j, 0)),
            pl.BlockSpec((1, D_MODEL), lambda i, j: (0, 0)),
            pl.BlockSpec((D_MODEL, QKV_WIDTH), lambda i, j: (0, 0), pipeline_mode=pl.Buffered(1)),
        ],
        out_specs=[
            pl.BlockSpec((1, tm, QKV_WIDTH), lambda i, j: (i, j, 0)),
            pl.BlockSpec((1, 4, tm // 4, DIL_QKV), lambda i, j: (i, 0, j, 0)),
            pl.BlockSpec((1, 16, tm // 16, DIL_QKV), lambda i, j: (i, 0, j, 0)),
        ],
        out_shape=[
            jax.ShapeDtypeStruct((b, t, QKV_WIDTH), jnp.bfloat16),
            jax.ShapeDtypeStruct((b, 4, t // 4, DIL_QKV), jnp.bfloat16),
            jax.ShapeDtypeStruct((b, 16, t // 16, DIL_QKV), jnp.bfloat16),
        ],
        scratch_shapes=[pltpu.VMEM((DIL_QKV // LANES, tm, LANES), jnp.float32),
                        pltpu.VMEM((tm, D_MODEL), jnp.bfloat16)],
        compiler_params=pltpu.CompilerParams(
            dimension_semantics=("parallel", "parallel"), vmem_limit_bytes=VMEM_LIMIT),
        name="qkv_proj",
    )(x, g, w)


def _low_lanes(shape):
    return lax.broadcasted_iota(jnp.int32, shape, len(shape) - 1) < HEAD_DIM


def _fill_value_planes(v, vaug_ref):
    low = _low_lanes(v.shape)
    one = jnp.ones_like(v)
    vaug_ref[0] = jnp.where(low, v, one)
    vaug_ref[1] = jnp.where(low, one, v)


def _staged_attention(nblk, batch, geom, load_q, load_k, load_v, bias, emit, s_ref, mx_ref):
    m = s_ref.shape[1]
    kwin = s_ref.shape[2]
    low = _low_lanes((m, LANES))

    def scores(i):
        qidx, kidx, var = geom(i)
        qb = load_q(qidx)
        zero = jnp.zeros_like(qb)
        q2 = jnp.concatenate([jnp.where(low, qb, zero), jnp.where(low, zero, qb)], axis=0)
        s2 = _dot_nt(q2, load_k(kidx))
        for h in range(2):
            s = s2[h * m:(h + 1) * m] + bias(var, h)
            s_ref[2 * i + h] = s
            mx_ref[2 * i + h] = jnp.broadcast_to(jnp.max(s, axis=-1, keepdims=True), (m, LANES))

    def values(i):
        qidx, kidx, _ = geom(i)
        pv = []
        for h in range(2):
            u = 2 * i + h
            mx = mx_ref[u]
            e = jnp.concatenate(
                [jnp.exp(s_ref[u, :, c * LANES:(c + 1) * LANES] - mx) for c in range(kwin // LANES)],
                axis=1)
            pv.append(_dot(e.astype(jnp.bfloat16), load_v(h, kidx)))
        num = jnp.where(low, pv[0], pv[1])
        den = pltpu.roll(jnp.where(low, pv[1], pv[0]), HEAD_DIM, 1)
        emit(qidx, num / den, jnp.where(low, mx_ref[2 * i], mx_ref[2 * i + 1]), den)

    assert nblk % batch == 0
    nbatch = nblk // batch
    for b in range(nbatch + 1):
        for i in range(batch):
            if b < nbatch:
                scores(b * batch + i)
            if b >= 1:
                values((b - 1) * batch + i)


def _attn_scratch(nblk, m, kwin, vaug_shape):
    return [
        pltpu.VMEM((2 * nblk, m, kwin), jnp.float32),
        pltpu.VMEM((2 * nblk, m, LANES), jnp.float32),
        pltpu.VMEM((2,) + vaug_shape, jnp.bfloat16),
    ]


def _na_kernel(q_ref, k_ref, v_ref, tab_ref, o_ref, s_ref, mx_ref, vaug_ref, *, rows):
    step = pl.program_id(2)

    @pl.when(step == 0)
    def _():
        _fill_value_planes(v_ref[0], vaug_ref)

    def geom(i):
        r = step * NA_STEP_ROWS + i
        rs = jnp.clip(r - NA_ROWS // 2, 0, rows - NA_ROWS)
        return (pl.multiple_of(i * GRID_W, GRID_W), pl.multiple_of(rs * GRID_W, GRID_W), r - rs)

    q0 = pl.multiple_of(step * (NA_STEP_ROWS * GRID_W), NA_STEP_ROWS * GRID_W)

    def emit(qidx, out, mx, den):
        o_ref[0, pl.ds(qidx, GRID_W), :] = out

    _staged_attention(
        NA_STEP_ROWS, NA_BATCH, geom,
        load_q=lambda qidx: q_ref[0, pl.ds(q0 + qidx, GRID_W), :],
        load_k=lambda kidx: k_ref[0, pl.ds(kidx, NA_ROWS * GRID_W), :],
        load_v=lambda h, kidx: vaug_ref[h, pl.ds(kidx, NA_ROWS * GRID_W), :],
        bias=lambda var, h: tab_ref[h, var],
        emit=emit, s_ref=s_ref, mx_ref=mx_ref)


def _na_call(nat, tab):
    b, t, _ = nat.shape
    rows = t // GRID_W
    kw = NA_ROWS * GRID_W
    col = lambda off: pl.BlockSpec((1, t, LANES), lambda i, j, s, off=off: (i, 0, off + j))
    return pl.pallas_call(
        functools.partial(_na_kernel, rows=rows),
        grid=(b, HEAD_PAIRS, rows // NA_STEP_ROWS),
        in_specs=[
            col(0), col(HEAD_PAIRS), col(2 * HEAD_PAIRS),
            pl.BlockSpec((2, NA_ROWS, GRID_W, kw), lambda i, j, s: (j, 0, 0, 0)),
        ],
        out_specs=pl.BlockSpec((1, NA_STEP_ROWS * GRID_W, LANES), lambda i, j, s: (i, s, j)),
        out_shape=jax.ShapeDtypeStruct((b, t, NA_WIDTH), jnp.float32),
        scratch_shapes=_attn_scratch(NA_STEP_ROWS, GRID_W, kw, (t, LANES)),
        compiler_params=pltpu.CompilerParams(
            dimension_semantics=("parallel", "parallel", "arbitrary"),
            vmem_limit_bytes=VMEM_LIMIT),
        name="na_attn",
    )(nat, nat, nat, tab)


def _dil_kernel(q_ref, k_ref, v_ref, tab_ref, o_ref, lse_ref, s_ref, mx_ref, vaug_ref,
                *, seq, group, kwin):
    nblk = seq // DIL_QBLK
    _fill_value_planes(v_ref[...], vaug_ref)

    def geom(i):
        g, q0 = i // nblk, (i % nblk) * DIL_QBLK
        k0 = min(max(q0 - DIL_HALF, 0), seq - kwin)
        return (g, q0), (g, k0), (q0 - k0) // DIL_HALF

    def emit(qidx, out, mx, den):
        g, q0 = qidx
        o_ref[g, pl.ds(q0, DIL_QBLK), :] = out
        lse_ref[g, pl.ds(q0, DIL_QBLK), :] = mx + jnp.log(den)

    _staged_attention(
        group * nblk, DIL_BATCH, geom,
        load_q=lambda qidx: q_ref[qidx[0], pl.ds(qidx[1], DIL_QBLK), :],
        load_k=lambda kidx: k_ref[kidx[0], pl.ds(kidx[1], kwin), :],
        load_v=lambda h, kidx: vaug_ref[h, kidx[0], pl.ds(kidx[1], kwin), :],
        bias=lambda var, h: tab_ref[var, h],
        emit=emit, s_ref=s_ref, mx_ref=mx_ref)


def _dil_tables(dilation, kwin):
    slopes = np.array([2.0 ** (-8.0 * (i + 1) / DIL_HEADS) for i in range(DIL_HEADS)], np.float32)
    i = np.arange(DIL_QBLK)[:, None]
    j = np.arange(kwin)[None, :]
    tabs = []
    for v in range((kwin - DIL_QBLK) // DIL_HALF + 1):
        diff = np.abs(j - i - v * DIL_HALF)
        dist = (dilation * diff).astype(np.float32)
        bias = -slopes[:, None, None] * dist[None]
        tabs.append(np.where((diff <= DIL_HALF)[None], bias, np.float32(NEG)))
    return jnp.asarray(np.stack(tabs).astype(np.float32))


def _dil_call(arr, col0, dilation):
    n, seq, _ = arr.shape
    kwin = min(DIL_KWIN, seq)
    group = max(1, min(n, DIL_STEP_TOKENS // seq))
    assert n % group == 0 and seq % DIL_QBLK == 0
    tab = _dil_tables(dilation, kwin)
    col = lambda off: pl.BlockSpec((group, seq, LANES), lambda i, j, off=off: (i, 0, off + j))
    out_spec = pl.BlockSpec((group, seq, LANES), lambda i, j: (i, 0, j))
    out_sds = jax.ShapeDtypeStruct((n, seq, DIL_WIDTH), jnp.float32)
    return pl.pallas_call(
        functools.partial(_dil_kernel, seq=seq, group=group, kwin=kwin),
        grid=(n // group, HEAD_PAIRS),
        in_specs=[
            col(col0), col(col0 + HEAD_PAIRS), col(col0 + 2 * HEAD_PAIRS),
            pl.BlockSpec((tab.shape[0], 2, DIL_QBLK, kwin), lambda i, j: (0, j, 0, 0)),
        ],
        out_specs=[out_spec, out_spec],
        out_shape=[out_sds, out_sds],
        scratch_shapes=_attn_scratch(group * seq // DIL_QBLK, DIL_QBLK, kwin, (group, seq, LANES)),
        compiler_params=pltpu.CompilerParams(
            dimension_semantics=("parallel", "parallel"), vmem_limit_bytes=VMEM_LIMIT),
        name=f"dil_attn_d{dilation}",
    )(arr, arr, arr, tab)


def _tail_kernel(x_ref, oa_ref, o1_ref, l1_ref, o4_ref, l4_ref, o16_ref, l16_ref,
                 gna_ref, gdil_ref, wout_ref, gffn_ref, wg_ref, wu_ref, wd_ref, gfin_ref,
                 y_ref, o4s, l4s, o16s, l16s, act_ref):
    tm = x_ref.shape[1]
    nslab = DIL_WIDTH // LANES
    for d, src, dst in ((4, o4_ref, o4s), (4, l4_ref, l4s), (16, o16_ref, o16s), (16, l16_ref, l16s)):
        rows = tm // d
        for p in range(d):
            for s in range(nslab):
                dst[s, pl.ds(p, rows, stride=d), :] = src[0, p, :, s * LANES:(s + 1) * LANES]

    od_parts = []
    for s in range(nslab):
        sl = slice(s * LANES, (s + 1) * LANES)
        l1, l4, l16 = l1_ref[0, :, sl], l4s[s], l16s[s]
        top = jnp.maximum(jnp.maximum(l1, l4), l16)
        w1, w4, w16 = jnp.exp(l1 - top), jnp.exp(l4 - top), jnp.exp(l16 - top)
        num = w1 * o1_ref[0, :, sl] + w4 * o4s[s] + w16 * o16s[s]
        od_parts.append(num / (w1 + w4 + w16))
    od = jnp.concatenate(od_parts, axis=-1)

    na = _rms(oa_ref[0], gna_ref[...]).astype(jnp.bfloat16)
    dn = _rms(od, gdil_ref[...]).astype(jnp.bfloat16)
    x2 = x_ref[0] + _dot(na, wout_ref[:NA_WIDTH, :]) + _dot(dn, wout_ref[NA_WIDTH:, :])

    hn = _rms(x2, gffn_ref[...]).astype(jnp.bfloat16)
    for c in range(D_FF // FFN_CHUNK):
        sl = slice(c * FFN_CHUNK, (c + 1) * FFN_CHUNK)
        gate = _dot(hn, wg_ref[:, sl])
        up = _dot(hn, wu_ref[:, sl])
        act_ref[:, sl] = (gate * jax.nn.sigmoid(gate) * up).astype(jnp.bfloat16)
    y_ref[0] = _rms(x2 + _dot(act_ref[...], wd_ref[...]), gfin_ref[...])


def _tail_call(x, oa, o1, l1, o4, l4, o16, l16, g_na, g_dil, w_out, g_ffn, w_gate, w_up, w_down,
               g_final):
    b, t, _ = x.shape
    tm = TAIL_TILE
    tok = lambda width: pl.BlockSpec((1, tm, width), lambda i, j: (i, j, 0))
    plane = lambda d: pl.BlockSpec((1, d, tm // d, DIL_WIDTH), lambda i, j: (i, 0, j, 0))
    const = lambda shape: pl.BlockSpec(shape, lambda i, j: (0,) * len(shape),
                                       pipeline_mode=pl.Buffered(1))
    slab = pltpu.VMEM((DIL_WIDTH // LANES, tm, LANES), jnp.float32)
    return pl.pallas_call(
        _tail_kernel,
        grid=(b, t // tm),
        in_specs=[
            tok(D_MODEL), tok(NA_WIDTH), tok(DIL_WIDTH), tok(DIL_WIDTH),
            plane(4), plane(4), plane(16), plane(16),
            const((1, NA_WIDTH)), const((1, DIL_WIDTH)), const((D_MODEL, D_MODEL)),
            const((1, D_MODEL)), const((D_MODEL, D_FF)), const((D_MODEL, D_FF)),
            const((D_FF, D_MODEL)), const((1, D_MODEL)),
        ],
        out_specs=tok(D_MODEL),
        out_shape=jax.ShapeDtypeStruct((b, t, D_MODEL), jnp.float32),
        scratch_shapes=[slab, slab, slab, slab, pltpu.VMEM((tm, D_FF), jnp.bfloat16)],
        compiler_params=pltpu.CompilerParams(
            dimension_semantics=("parallel", "parallel"), vmem_limit_bytes=VMEM_LIMIT),
        name="tail",
    )(x, oa, o1, l1, o4, l4, o16, l16, g_na, g_dil, w_out, g_ffn, w_gate, w_up, w_down, g_final)


def _na_table(rpb):
    qc = np.arange(GRID_W)[:, None]
    kc = np.arange(GRID_W)[None, :]
    wstart = np.clip(qc - NA_COLS // 2, 0, GRID_W - NA_COLS)
    ok = (kc >= wstart) & (kc < wstart + NA_COLS)
    relc = np.clip(kc - qc + NA_COLS - 1, 0, 2 * NA_COLS - 2)
    onehot = (relc[None] == np.arange(2 * NA_COLS - 1)[:, None, None]).astype(np.float32)
    colbias = jnp.einsum("hrc,cqk->hrqk", rpb.astype(jnp.float32), jnp.asarray(onehot),
                         precision=lax.Precision.HIGHEST)
    colbias = jnp.where(jnp.asarray(ok)[None, None], colbias, NEG)
    variants = []
    for a in range(NA_ROWS):
        win = colbias[:, NA_ROWS - 1 - a:2 * NA_ROWS - 1 - a]
        variants.append(win.transpose(0, 2, 1, 3).reshape(NA_HEADS, GRID_W, NA_ROWS * GRID_W))
    return jnp.stack(variants, axis=1)


def _trunk(x, w_in, na_tab, g_attn, g_na, g_dil, w_out, g_ffn, w_gate, w_up, w_down, g_final):
    b, t, _ = x.shape
    nat, d4, d16 = _qkv_call(x, g_attn, w_in)
    oa = _na_call(nat, na_tab)
    o1, l1 = _dil_call(nat, (NA_WIDTH * 3) // LANES, 1)
    o4, l4 = _dil_call(d4.reshape(b * 4, t // 4, DIL_QKV), 0, 4)
    o16, l16 = _dil_call(d16.reshape(b * 16, t // 16, DIL_QKV), 0, 16)
    plane = lambda z, d: z.reshape(b, d, t // d, DIL_WIDTH)
    return _tail_call(x, oa, o1, l1, plane(o4, 4), plane(l4, 4), plane(o16, 16), plane(l16, 16),
                      g_na, g_dil, w_out, g_ffn, w_gate, w_up, w_down, g_final)


def kernel(x_prompt, x_sample, w_in, rpb, g_attn, g_na, g_dil, w_out, g_ffn, w_gate, w_up, w_down,
           g_final):
    assert w_in.shape[0] == 1, "single-layer trunk"
    bf = lambda w: w[0].astype(jnp.bfloat16)
    params = (bf(w_in), _na_table(rpb[0]), g_attn, g_na, g_dil, bf(w_out), g_ffn, bf(w_gate), bf(w_up),
              bf(w_down), g_final.reshape(1, D_MODEL))
    return _trunk(x_prompt, *params), _trunk(x_sample, *params)
```

```python
import functools

import jax
import jax.numpy as jnp
import numpy as np
from jax import lax
from jax.experimental import pallas as pl
from jax.experimental.pallas import tpu as pltpu

D_MODEL = 1024
HEAD_DIM = 64
NA_HEADS = 8
DIL_HEADS = 8
NA_WIDTH = NA_HEADS * HEAD_DIM
DIL_WIDTH = DIL_HEADS * HEAD_DIM
QKV_WIDTH = 3 * (NA_WIDTH + DIL_WIDTH)
DIL_QKV = 3 * DIL_WIDTH
D_FF = 2816
GRID_W = 64
NA_ROWS = 8
NA_COLS = 16
DIL_PATTERNS = ((128, 1), (512, 4), (2048, 16))
DIL_HALF = 64
RMS_EPS = 1e-6
NEG = -1e30
LOG2E = 1.4426950408889634
LN2 = 0.6931471805599453
Q_SCALE = HEAD_DIM ** -0.5 * LOG2E

LANES = 128
HEAD_PAIRS = NA_HEADS // 2
QKV_TILE = 512
TAIL_TILE = 512
DIL_QBLK = 128
DIL_KWIN = 256
DIL_STEP_TOKENS = 4096
FFN_CHUNK = 256
QKV_CHUNK = 256
NA_STEP_ROWS = 32
NA_BATCH = 4
DIL_BATCH = 4
VMEM_LIMIT = 56 * 1024 * 1024

assert all(w // (2 * d) == DIL_HALF for w, d in DIL_PATTERNS)


def _rms(x, g):
    return x * lax.rsqrt(jnp.mean(x * x, axis=-1, keepdims=True) + RMS_EPS) * g


def _dot_nt(a, b):
    return lax.dot_general(a, b, (((1,), (1,)), ((), ())), preferred_element_type=jnp.float32)


def _dot(a, b):
    return jnp.dot(a, b, preferred_element_type=jnp.float32)


def _qkv_kernel(x_ref, g_ref, w_ref, nat_ref, d4_ref, d16_ref, slab_ref, slab4_ref, hn_ref):
    tm = x_ref.shape[1]

    hn_ref[...] = _rms(x_ref[0], g_ref[...]).astype(jnp.bfloat16)
    chunk = QKV_CHUNK
    slabs = chunk // LANES
    nchunk = QKV_WIDTH // chunk
    per_role = NA_WIDTH // chunk
    for c in list(range(nchunk // 2, nchunk)) + list(range(nchunk // 2)):
        acc = _dot(hn_ref[...], w_ref[:, c * chunk:(c + 1) * chunk])
        if (c // per_role) % 3 == 0:
            acc = acc * Q_SCALE
        nat_ref[0, :, c * chunk:(c + 1) * chunk] = acc.astype(jnp.bfloat16)
        if c < nchunk // 2:
            continue
        s0 = (c - nchunk // 2) * slabs
        for j in range(slabs):
            slab_ref[s0 + j] = acc[:, j * LANES:(j + 1) * LANES]
        r4, r16 = tm // 4, tm // 16
        for s in range(s0, s0 + slabs):
            lanes = slice(s * LANES, (s + 1) * LANES)
            for w in range(4):
                plane = slab_ref[s, pl.ds(w, r4, stride=4), :]
                d4_ref[0, w, :, lanes] = plane.astype(jnp.bfloat16)
                slab4_ref[s, w * r4:(w + 1) * r4, :] = plane
            for w in range(4):
                for u in range(4):
                    d16_ref[0, 4 * w + u, :, lanes] = (
                        slab4_ref[s, pl.ds(w * r4 + u, r16, stride=4), :].astype(jnp.bfloat16))


def _qkv_call(x, g, w):
    b, t, _ = x.shape
    tm = QKV_TILE
    return pl.pallas_call(
        _qkv_kernel,
        grid=(b, t // tm),
        in_specs=[
            pl.BlockSpec((1, tm, D_MODEL), lambda i, j: (i, j, 0)),
            pl.BlockSpec((1, D_MODEL), lambda i, j: (0, 0)),
            pl.BlockSpec((D_MODEL, QKV_WIDTH), lambda i, j: (0, 0), pipeline_mode=pl.Buffered(1)),
        ],
        out_specs=[
            pl.BlockSpec((1, tm, QKV_WIDTH), lambda i, j: (i, j, 0)),
            pl.BlockSpec((1, 4, tm // 4, DIL_QKV), lambda i, j: (i, 0, j, 0)),
            pl.BlockSpec((1, 16, tm // 16, DIL_QKV), lambda i, j: (i, 0, j, 0)),
        ],
        out_shape=[
            jax.ShapeDtypeStruct((b, t, QKV_WIDTH), jnp.bfloat16),
            jax.ShapeDtypeStruct((b, 4, t // 4, DIL_QKV), jnp.bfloat16),
            jax.ShapeDtypeStruct((b, 16, t // 16, DIL_QKV), jnp.bfloat16),
        ],
        scratch_shapes=[pltpu.VMEM((DIL_QKV // LANES, tm, LANES), jnp.float32),
                        pltpu.VMEM((DIL_QKV // LANES, tm, LANES), jnp.float32),
                        pltpu.VMEM((tm, D_MODEL), jnp.bfloat16)],
        compiler_params=pltpu.CompilerParams(
            dimension_semantics=("parallel", "parallel"), vmem_limit_bytes=VMEM_LIMIT),
        name="qkv_proj",
    )(x, g, w)


def _low_lanes(shape):
    return lax.broadcasted_iota(jnp.int32, shape, len(shape) - 1) < HEAD_DIM


def _fill_value_planes(v, vaug_ref):
    low = _low_lanes(v.shape)
    one = jnp.ones_like(v)
    vaug_ref[0] = jnp.where(low, v, one)
    vaug_ref[1] = jnp.where(low, one, v)


def _staged_attention(nblk, batch, geom, load_q, load_k, load_v, bias, emit, s_ref, mx_ref):
    m = s_ref.shape[1]
    kwin = s_ref.shape[2]
    low = _low_lanes((m, LANES))

    def scores(i):
        qidx, kidx, var = geom(i)
        qb = load_q(qidx)
        zero = jnp.zeros_like(qb)
        q2 = jnp.concatenate([jnp.where(low, qb, zero), jnp.where(low, zero, qb)], axis=0)
        s2 = _dot_nt(q2, load_k(kidx))
        for h in range(2):
            s = s2[h * m:(h + 1) * m] + bias(var, h)
            s_ref[2 * i + h] = s
            mx_ref[2 * i + h] = jnp.broadcast_to(jnp.max(s, axis=-1, keepdims=True), (m, LANES))

    def values(i):
        qidx, kidx, _ = geom(i)
        pv = []
        for h in range(2):
            u = 2 * i + h
            mx = mx_ref[u]
            e = jnp.concatenate(
                [jnp.exp2(s_ref[u, :, c * LANES:(c + 1) * LANES] - mx) for c in range(kwin // LANES)],
                axis=1)
            pv.append(_dot(e.astype(jnp.bfloat16), load_v(h, kidx)))
        num = jnp.where(low, pv[0], pv[1])
        den = pltpu.roll(jnp.where(low, pv[1], pv[0]), HEAD_DIM, 1)
        emit(qidx, num / den, jnp.where(low, mx_ref[2 * i], mx_ref[2 * i + 1]), den)

    assert nblk % batch == 0
    nbatch = nblk // batch
    for b in range(nbatch + 1):
        for i in range(batch):
            if b < nbatch:
                scores(b * batch + i)
            if b >= 1:
                values((b - 1) * batch + i)


def _attn_scratch(nblk, m, kwin, vaug_shape):
    return [
        pltpu.VMEM((2 * nblk, m, kwin), jnp.float32),
        pltpu.VMEM((2 * nblk, m, LANES), jnp.float32),
        pltpu.VMEM((2,) + vaug_shape, jnp.bfloat16),
    ]


def _na_kernel(q_ref, k_ref, v_ref, tab_ref, o_ref, s_ref, mx_ref, vaug_ref, *, rows):
    step = pl.program_id(2)

    @pl.when(step == 0)
    def _():
        _fill_value_planes(v_ref[0], vaug_ref)

    def geom(i):
        r = step * NA_STEP_ROWS + i
        rs = jnp.clip(r - NA_ROWS // 2, 0, rows - NA_ROWS)
        return (pl.multiple_of(i * GRID_W, GRID_W), pl.multiple_of(rs * GRID_W, GRID_W), r - rs)

    q0 = pl.multiple_of(step * (NA_STEP_ROWS * GRID_W), NA_STEP_ROWS * GRID_W)

    def emit(qidx, out, mx, den):
        o_ref[0, pl.ds(qidx, GRID_W), :] = out

    _staged_attention(
        NA_STEP_ROWS, NA_BATCH, geom,
        load_q=lambda qidx: q_ref[0, pl.ds(q0 + qidx, GRID_W), :],
        load_k=lambda kidx: k_ref[0, pl.ds(kidx, NA_ROWS * GRID_W), :],
        load_v=lambda h, kidx: vaug_ref[h, pl.ds(kidx, NA_ROWS * GRID_W), :],
        bias=lambda var, h: tab_ref[h, var],
        emit=emit, s_ref=s_ref, mx_ref=mx_ref)


def _na_call(nat, tab):
    b, t, _ = nat.shape
    rows = t // GRID_W
    kw = NA_ROWS * GRID_W
    col = lambda off: pl.BlockSpec((1, t, LANES), lambda j, i, s, off=off: (i, 0, off + j))
    return pl.pallas_call(
        functools.partial(_na_kernel, rows=rows),
        grid=(HEAD_PAIRS, b, rows // NA_STEP_ROWS),
        in_specs=[
            col(0), col(HEAD_PAIRS), col(2 * HEAD_PAIRS),
            pl.BlockSpec((2, NA_ROWS, GRID_W, kw), lambda j, i, s: (j, 0, 0, 0)),
        ],
        out_specs=pl.BlockSpec((1, NA_STEP_ROWS * GRID_W, LANES), lambda j, i, s: (i, s, j)),
        out_shape=jax.ShapeDtypeStruct((b, t, NA_WIDTH), jnp.float32),
        scratch_shapes=_attn_scratch(NA_STEP_ROWS, GRID_W, kw, (t, LANES)),
        compiler_params=pltpu.CompilerParams(
            dimension_semantics=("parallel", "parallel", "arbitrary"),
            vmem_limit_bytes=VMEM_LIMIT),
        name="na_attn",
    )(nat, nat, nat, tab)


def _dil_kernel(q_ref, k_ref, v_ref, tab_ref, o_ref, lse_ref, s_ref, mx_ref, vaug_ref,
                *, seq, group, kwin):
    nblk = seq // DIL_QBLK
    _fill_value_planes(v_ref[...], vaug_ref)

    def geom(i):
        g, q0 = i // nblk, (i % nblk) * DIL_QBLK
        k0 = min(max(q0 - DIL_HALF, 0), seq - kwin)
        return (g, q0), (g, k0), (q0 - k0) // DIL_HALF

    def emit(qidx, out, mx, den):
        g, q0 = qidx
        o_ref[g, pl.ds(q0, DIL_QBLK), :] = out
        lse_ref[g, pl.ds(q0, DIL_QBLK), :] = mx * LN2 + jnp.log(den)

    _staged_attention(
        group * nblk, DIL_BATCH, geom,
        load_q=lambda qidx: q_ref[qidx[0], pl.ds(qidx[1], DIL_QBLK), :],
        load_k=lambda kidx: k_ref[kidx[0], pl.ds(kidx[1], kwin), :],
        load_v=lambda h, kidx: vaug_ref[h, kidx[0], pl.ds(kidx[1], kwin), :],
        bias=lambda var, h: tab_ref[var, h],
        emit=emit, s_ref=s_ref, mx_ref=mx_ref)


def _dil_tables(dilation, kwin):
    slopes = np.array([2.0 ** (-8.0 * (i + 1) / DIL_HEADS) for i in range(DIL_HEADS)], np.float32)
    i = np.arange(DIL_QBLK)[:, None]
    j = np.arange(kwin)[None, :]
    tabs = []
    for v in range((kwin - DIL_QBLK) // DIL_HALF + 1):
        diff = np.abs(j - i - v * DIL_HALF)
        dist = (dilation * diff).astype(np.float32)
        bias = -slopes[:, None, None] * dist[None]
        tabs.append(np.where((diff <= DIL_HALF)[None], bias * np.float32(LOG2E), np.float32(NEG)))
    return jnp.asarray(np.stack(tabs).astype(np.float32))


def _dil_call(arr, col0, dilation):
    n, seq, _ = arr.shape
    kwin = min(DIL_KWIN, seq)
    group = max(1, min(n, DIL_STEP_TOKENS // seq))
    assert n % group == 0 and seq % DIL_QBLK == 0
    tab = _dil_tables(dilation, kwin)
    col = lambda off: pl.BlockSpec((group, seq, LANES), lambda j, i, off=off: (i, 0, off + j))
    out_spec = pl.BlockSpec((group, seq, LANES), lambda j, i: (i, 0, j))
    out_sds = jax.ShapeDtypeStruct((n, seq, DIL_WIDTH), jnp.float32)
    return pl.pallas_call(
        functools.partial(_dil_kernel, seq=seq, group=group, kwin=kwin),
        grid=(HEAD_PAIRS, n // group),
        in_specs=[
            col(col0), col(col0 + HEAD_PAIRS), col(col0 + 2 * HEAD_PAIRS),
            pl.BlockSpec((tab.shape[0], 2, DIL_QBLK, kwin), lambda j, i: (0, j, 0, 0)),
        ],
        out_specs=[out_spec, out_spec],
        out_shape=[out_sds, out_sds],
        scratch_shapes=_attn_scratch(group * seq // DIL_QBLK, DIL_QBLK, kwin, (group, seq, LANES)),
        compiler_params=pltpu.CompilerParams(
            dimension_semantics=("parallel", "parallel"), vmem_limit_bytes=VMEM_LIMIT),
        name=f"dil_attn_d{dilation}",
    )(arr, arr, arr, tab)


def _tail_kernel(x_ref, oa_ref, o1_ref, l1_ref, o4_ref, l4_ref, o16_ref, l16_ref,
                 gna_ref, gdil_ref, wout_ref, gffn_ref, wg_ref, wu_ref, wd_ref, gfin_ref,
                 y_ref, o4s, l4s, o16s, l16s, tmp_ref, act_ref):
    tm = x_ref.shape[1]
    nslab = DIL_WIDTH // LANES
    r4, r16 = tm // 4, tm // 16
    for src, dst in ((o16_ref, o16s), (l16_ref, l16s)):
        for s in range(nslab):
            lanes = slice(s * LANES, (s + 1) * LANES)
            for w in range(4):
                for u in range(4):
                    tmp_ref[s, pl.ds(w * r4 + u, r16, stride=4), :] = src[0, 4 * w + u, :, lanes]
            for w in range(4):
                dst[s, pl.ds(w, r4, stride=4), :] = tmp_ref[s, w * r4:(w + 1) * r4, :]
    for src, dst in ((o4_ref, o4s), (l4_ref, l4s)):
        for s in range(nslab):
            for w in range(4):
                dst[s, pl.ds(w, r4, stride=4), :] = src[0, w, :, s * LANES:(s + 1) * LANES]

    od_parts = []
    for s in range(nslab):
        sl = slice(s * LANES, (s + 1) * LANES)
        l1, l4, l16 = l1_ref[0, :, sl], l4s[s], l16s[s]
        top = jnp.maximum(jnp.maximum(l1, l4), l16)
        w1, w4, w16 = jnp.exp(l1 - top), jnp.exp(l4 - top), jnp.exp(l16 - top)
        num = w1 * o1_ref[0, :, sl] + w4 * o4s[s] + w16 * o16s[s]
        od_parts.append(num / (w1 + w4 + w16))
    od = jnp.concatenate(od_parts, axis=-1)

    na = _rms(oa_ref[0], gna_ref[...]).astype(jnp.bfloat16)
    dn = _rms(od, gdil_ref[...]).astype(jnp.bfloat16)
    x2 = x_ref[0] + _dot(na, wout_ref[:NA_WIDTH, :]) + _dot(dn, wout_ref[NA_WIDTH:, :])

    hn = _rms(x2, gffn_ref[...]).astype(jnp.bfloat16)
    for c in range(D_FF // FFN_CHUNK):
        sl = slice(c * FFN_CHUNK, (c + 1) * FFN_CHUNK)
        gate = _dot(hn, wg_ref[:, sl])
        up = _dot(hn, wu_ref[:, sl])
        act_ref[:, sl] = (gate * jax.nn.sigmoid(gate) * up).astype(jnp.bfloat16)
    y_ref[0] = _rms(x2 + _dot(act_ref[...], wd_ref[...]), gfin_ref[...])


def _tail_call(x, oa, o1, l1, o4, l4, o16, l16, g_na, g_dil, w_out, g_ffn, w_gate, w_up, w_down,
               g_final):
    b, t, _ = x.shape
    tm = TAIL_TILE
    tok = lambda width: pl.BlockSpec((1, tm, width), lambda i, j: (i, j, 0))
    plane = lambda d: pl.BlockSpec((1, d, tm // d, DIL_WIDTH), lambda i, j: (i, 0, j, 0))
    const = lambda shape: pl.BlockSpec(shape, lambda i, j: (0,) * len(shape),
                                       pipeline_mode=pl.Buffered(1))
    slab = pltpu.VMEM((DIL_WIDTH // LANES, tm, LANES), jnp.float32)
    return pl.pallas_call(
        _tail_kernel,
        grid=(b, t // tm),
        in_specs=[
            tok(D_MODEL), tok(NA_WIDTH), tok(DIL_WIDTH), tok(DIL_WIDTH),
            plane(4), plane(4), plane(16), plane(16),
            const((1, NA_WIDTH)), const((1, DIL_WIDTH)), const((D_MODEL, D_MODEL)),
            const((1, D_MODEL)), const((D_MODEL, D_FF)), const((D_MODEL, D_FF)),
            const((D_FF, D_MODEL)), const((1, D_MODEL)),
        ],
        out_specs=tok(D_MODEL),
        out_shape=jax.ShapeDtypeStruct((b, t, D_MODEL), jnp.float32),
        scratch_shapes=[slab, slab, slab, slab, slab, pltpu.VMEM((tm, D_FF), jnp.bfloat16)],
        compiler_params=pltpu.CompilerParams(
            dimension_semantics=("parallel", "parallel"), vmem_limit_bytes=VMEM_LIMIT),
        name="tail",
    )(x, oa, o1, l1, o4, l4, o16, l16, g_na, g_dil, w_out, g_ffn, w_gate, w_up, w_down, g_final)


def _na_table(rpb):
    qc = np.arange(GRID_W)[:, None]
    kc = np.arange(GRID_W)[None, :]
    wstart = np.clip(qc - NA_COLS // 2, 0, GRID_W - NA_COLS)
    ok = (kc >= wstart) & (kc < wstart + NA_COLS)
    relc = np.clip(kc - qc + NA_COLS - 1, 0, 2 * NA_COLS - 2)
    onehot = (relc[None] == np.arange(2 * NA_COLS - 1)[:, None, None]).astype(np.float32)
    colbias = jnp.einsum("hrc,cqk->hrqk", rpb.astype(jnp.float32), jnp.asarray(onehot),
                         precision=lax.Precision.HIGHEST)
    colbias = jnp.where(jnp.asarray(ok)[None, None], colbias * LOG2E, NEG)
    variants = []
    for a in range(NA_ROWS):
        win = colbias[:, NA_ROWS - 1 - a:2 * NA_ROWS - 1 - a]
        variants.append(win.transpose(0, 2, 1, 3).reshape(NA_HEADS, GRID_W, NA_ROWS * GRID_W))
    return jnp.stack(variants, axis=1)


def _trunk(x, w_in, na_tab, g_attn, g_na, g_dil, w_out, g_ffn, w_gate, w_up, w_down, g_final):
    b, t, _ = x.shape
    nat, d4, d16 = _qkv_call(x, g_attn, w_in)
    oa = _na_call(nat, na_tab)
    o1, l1 = _dil_call(nat, (NA_WIDTH * 3) // LANES, 1)
    o4, l4 = _dil_call(d4.reshape(b * 4, t // 4, DIL_QKV), 0, 4)
    o16, l16 = _dil_call(d16.reshape(b * 16, t // 16, DIL_QKV), 0, 16)
    plane = lambda z, d: z.reshape(b, d, t // d, DIL_WIDTH)
    return _tail_call(x, oa, o1, l1, plane(o4, 4), plane(l4, 4), plane(o16, 16), plane(l16, 16),
                      g_na, g_dil, w_out, g_ffn, w_gate, w_up, w_down, g_final)


def kernel(x_prompt, x_sample, w_in, rpb, g_attn, g_na, g_dil, w_out, g_ffn, w_gate, w_up, w_down,
           g_final):
    assert w_in.shape[0] == 1, "single-layer trunk"
    bf = lambda w: w[0].astype(jnp.bfloat16)
    params = (bf(w_in), _na_table(rpb[0]), g_attn, g_na, g_dil, bf(w_out), g_ffn, bf(w_gate), bf(w_up),
              bf(w_down), g_final.reshape(1, D_MODEL))
    return _trunk(x_prompt, *params), _trunk(x_sample, *params)
```

```python
import functools

import jax
import jax.numpy as jnp
import numpy as np
from jax import lax
from jax.experimental import pallas as pl
from jax.experimental.pallas import tpu as pltpu

D_MODEL = 1024
HEAD_DIM = 64
NA_HEADS = 8
DIL_HEADS = 8
NA_WIDTH = NA_HEADS * HEAD_DIM
DIL_WIDTH = DIL_HEADS * HEAD_DIM
QKV_WIDTH = 3 * (NA_WIDTH + DIL_WIDTH)
DIL_QKV = 3 * DIL_WIDTH
D_FF = 2816
GRID_W = 64
NA_ROWS = 8
NA_COLS = 16
DIL_PATTERNS = ((128, 1), (512, 4), (2048, 16))
DIL_HALF = 64
RMS_EPS = 1e-6
NEG = -1e30
LOG2E = 1.4426950408889634
LN2 = 0.6931471805599453
Q_SCALE = HEAD_DIM ** -0.5 * LOG2E

LANES = 128
HEAD_PAIRS = NA_HEADS // 2
QKV_TILE = 512
TAIL_TILE = 512
DIL_QBLK = 128
DIL_KWIN = 256
DIL_STEP_TOKENS = 4096
FFN_CHUNK = 256
QKV_CHUNK = 256
NA_STEP_ROWS = 32
NA_BATCH = 4
DIL_BATCH = 4
ATTN_RING = 2
VMEM_LIMIT = 56 * 1024 * 1024

assert all(w // (2 * d) == DIL_HALF for w, d in DIL_PATTERNS)


def _rms(x, g):
    return x * lax.rsqrt(jnp.mean(x * x, axis=-1, keepdims=True) + RMS_EPS) * g


def _dot_nt(a, b):
    return lax.dot_general(a, b, (((1,), (1,)), ((), ())), preferred_element_type=jnp.float32)


def _dot(a, b):
    return jnp.dot(a, b, preferred_element_type=jnp.float32)


def _qkv_kernel(x_ref, g_ref, w_ref, nat_ref, d4_ref, d16_ref, slab_ref, slab4_ref, hn_ref):
    tm = x_ref.shape[1]

    hn_ref[...] = _rms(x_ref[0], g_ref[...]).astype(jnp.bfloat16)
    chunk = QKV_CHUNK
    slabs = chunk // LANES
    nchunk = QKV_WIDTH // chunk
    per_role = NA_WIDTH // chunk
    for c in list(range(nchunk // 2, nchunk)) + list(range(nchunk // 2)):
        acc = _dot(hn_ref[...], w_ref[:, c * chunk:(c + 1) * chunk])
        if (c // per_role) % 3 == 0:
            acc = acc * Q_SCALE
        nat_ref[0, :, c * chunk:(c + 1) * chunk] = acc.astype(jnp.bfloat16)
        if c < nchunk // 2:
            continue
        s0 = (c - nchunk // 2) * slabs
        for j in range(slabs):
            slab_ref[s0 + j] = acc[:, j * LANES:(j + 1) * LANES]
        r4, r16 = tm // 4, tm // 16
        for s in range(s0, s0 + slabs):
            lanes = slice(s * LANES, (s + 1) * LANES)
            for w in range(4):
                plane = slab_ref[s, pl.ds(w, r4, stride=4), :]
                d4_ref[0, w, :, lanes] = plane.astype(jnp.bfloat16)
                slab4_ref[s, w * r4:(w + 1) * r4, :] = plane
            for w in range(4):
                for u in range(4):
                    d16_ref[0, 4 * w + u, :, lanes] = (
                        slab4_ref[s, pl.ds(w * r4 + u, r16, stride=4), :].astype(jnp.bfloat16))


def _qkv_call(x, g, w):
    b, t, _ = x.shape
    tm = QKV_TILE
    return pl.pallas_call(
        _qkv_kernel,
        grid=(b, t // tm),
        in_specs=[
            pl.BlockSpec((1, tm, D_MODEL), lambda i, j: (i, j, 0)),
            pl.BlockSpec((1, D_MODEL), lambda i, j: (0, 0)),
            pl.BlockSpec((D_MODEL, QKV_WIDTH), lambda i, j: (0, 0), pipeline_mode=pl.Buffered(1)),
        ],
        out_specs=[
            pl.BlockSpec((1, tm, QKV_WIDTH), lambda i, j: (i, j, 0)),
            pl.BlockSpec((1, 4, tm // 4, DIL_QKV), lambda i, j: (i, 0, j, 0)),
            pl.BlockSpec((1, 16, tm // 16, DIL_QKV), lambda i, j: (i, 0, j, 0)),
        ],
        out_shape=[
            jax.ShapeDtypeStruct((b, t, QKV_WIDTH), jnp.bfloat16),
            jax.ShapeDtypeStruct((b, 4, t // 4, DIL_QKV), jnp.bfloat16),
            jax.ShapeDtypeStruct((b, 16, t // 16, DIL_QKV), jnp.bfloat16),
        ],
        scratch_shapes=[pltpu.VMEM((DIL_QKV // LANES, tm, LANES), jnp.float32),
                        pltpu.VMEM((DIL_QKV // LANES, tm, LANES), jnp.float32),
                        pltpu.VMEM((tm, D_MODEL), jnp.bfloat16)],
        compiler_params=pltpu.CompilerParams(
            dimension_semantics=("parallel", "parallel"), vmem_limit_bytes=VMEM_LIMIT),
        name="qkv_proj",
    )(x, g, w)


def _low_lanes(shape):
    return lax.broadcasted_iota(jnp.int32, shape, len(shape) - 1) < HEAD_DIM


def _fill_value_planes(v, vaug_ref):
    low = _low_lanes(v.shape)
    one = jnp.ones_like(v)
    vaug_ref[0] = jnp.where(low, v, one)
    vaug_ref[1] = jnp.where(low, one, v)


def _staged_attention(nblk, batch, stacked_scores, geom, load_q, load_k, load_v, bias, emit,
                      s_ref, mx_ref):
    m = s_ref.shape[1]
    kwin = s_ref.shape[2]
    low = _low_lanes((m, LANES))
    ring = s_ref.shape[0] // 2
    assert ring >= 2 * batch or ring >= nblk

    def scores(i):
        qidx, kidx, var = geom(i)
        qb = load_q(qidx)
        zero = jnp.zeros_like(qb)
        kb = load_k(kidx)
        qh = (jnp.where(low, qb, zero), jnp.where(low, zero, qb))
        if stacked_scores:
            s2 = _dot_nt(jnp.concatenate(qh, axis=0), kb)
            sh = (s2[:m], s2[m:])
        else:
            sh = (_dot_nt(qh[0], kb), _dot_nt(qh[1], kb))
        for h in range(2):
            u = 2 * (i % ring) + h
            s = sh[h] + bias(var, h)
            s_ref[u] = s
            mx_ref[u] = jnp.broadcast_to(jnp.max(s, axis=-1, keepdims=True), (m, LANES))

    def values(i):
        qidx, kidx, _ = geom(i)
        u0 = 2 * (i % ring)
        pv = []
        for h in range(2):
            mx = mx_ref[u0 + h]
            e = jnp.concatenate(
                [jnp.exp2(s_ref[u0 + h, :, c * LANES:(c + 1) * LANES] - mx)
                 for c in range(kwin // LANES)], axis=1)
            pv.append(_dot(e.astype(jnp.bfloat16), load_v(h, kidx)))
        num = jnp.where(low, pv[0], pv[1])
        den = pltpu.roll(jnp.where(low, pv[1], pv[0]), HEAD_DIM, 1)
        emit(qidx, num / den, jnp.where(low, mx_ref[u0], mx_ref[u0 + 1]), den)

    assert nblk % batch == 0
    nbatch = nblk // batch
    for b in range(nbatch + 1):
        for i in range(batch):
            if b < nbatch:
                scores(b * batch + i)
            if b >= 1:
                values((b - 1) * batch + i)


def _attn_scratch(ring, m, kwin, vaug_shape):
    return [
        pltpu.VMEM((2 * ring, m, kwin), jnp.float32),
        pltpu.VMEM((2 * ring, m, LANES), jnp.float32),
        pltpu.VMEM((2,) + vaug_shape, jnp.bfloat16),
    ]


def _na_kernel(q_ref, k_ref, v_ref, tab_ref, o_ref, s_ref, mx_ref, vaug_ref, *, rows):
    step = pl.program_id(2)

    @pl.when(step == 0)
    def _():
        _fill_value_planes(v_ref[0], vaug_ref)

    def geom(i):
        r = step * NA_STEP_ROWS + i
        rs = jnp.clip(r - NA_ROWS // 2, 0, rows - NA_ROWS)
        return (pl.multiple_of(i * GRID_W, GRID_W), pl.multiple_of(rs * GRID_W, GRID_W), r - rs)

    q0 = pl.multiple_of(step * (NA_STEP_ROWS * GRID_W), NA_STEP_ROWS * GRID_W)

    def emit(qidx, out, mx, den):
        o_ref[0, pl.ds(qidx, GRID_W), :] = out

    _staged_attention(
        NA_STEP_ROWS, NA_BATCH, True, geom,
        load_q=lambda qidx: q_ref[0, pl.ds(q0 + qidx, GRID_W), :],
        load_k=lambda kidx: k_ref[0, pl.ds(kidx, NA_ROWS * GRID_W), :],
        load_v=lambda h, kidx: vaug_ref[h, pl.ds(kidx, NA_ROWS * GRID_W), :],
        bias=lambda var, h: tab_ref[h, var],
        emit=emit, s_ref=s_ref, mx_ref=mx_ref)


def _na_call(nat, tab):
    b, t, _ = nat.shape
    rows = t // GRID_W
    kw = NA_ROWS * GRID_W
    col = lambda off: pl.BlockSpec((1, t, LANES), lambda j, i, s, off=off: (i, 0, off + j))
    return pl.pallas_call(
        functools.partial(_na_kernel, rows=rows),
        grid=(HEAD_PAIRS, b, rows // NA_STEP_ROWS),
        in_specs=[
            col(0), col(HEAD_PAIRS), col(2 * HEAD_PAIRS),
            pl.BlockSpec((2, NA_ROWS, GRID_W, kw), lambda j, i, s: (j, 0, 0, 0)),
        ],
        out_specs=pl.BlockSpec((1, NA_STEP_ROWS * GRID_W, LANES), lambda j, i, s: (i, s, j)),
        out_shape=jax.ShapeDtypeStruct((b, t, NA_WIDTH), jnp.float32),
        scratch_shapes=_attn_scratch(min(NA_STEP_ROWS, ATTN_RING * NA_BATCH), GRID_W, kw, (t, LANES)),
        compiler_params=pltpu.CompilerParams(
            dimension_semantics=("parallel", "parallel", "arbitrary"),
            vmem_limit_bytes=VMEM_LIMIT),
        name="na_attn",
    )(nat, nat, nat, tab)


def _dil_kernel(q_ref, k_ref, v_ref, tab_ref, o_ref, lse_ref, s_ref, mx_ref, vaug_ref,
                *, seq, group, kwin):
    nblk = seq // DIL_QBLK
    _fill_value_planes(v_ref[...], vaug_ref)

    def geom(i):
        g, q0 = i // nblk, (i % nblk) * DIL_QBLK
        k0 = min(max(q0 - DIL_HALF, 0), seq - kwin)
        return (g, q0), (g, k0), (q0 - k0) // DIL_HALF

    def emit(qidx, out, mx, den):
        g, q0 = qidx
        o_ref[g, pl.ds(q0, DIL_QBLK), :] = out
        lse_ref[g, pl.ds(q0, DIL_QBLK), :] = mx * LN2 + jnp.log(den)

    _staged_attention(
        group * nblk, DIL_BATCH, False, geom,
        load_q=lambda qidx: q_ref[qidx[0], pl.ds(qidx[1], DIL_QBLK), :],
        load_k=lambda kidx: k_ref[kidx[0], pl.ds(kidx[1], kwin), :],
        load_v=lambda h, kidx: vaug_ref[h, kidx[0], pl.ds(kidx[1], kwin), :],
        bias=lambda var, h: tab_ref[var, h],
        emit=emit, s_ref=s_ref, mx_ref=mx_ref)


def _dil_tables(dilation, kwin):
    slopes = np.array([2.0 ** (-8.0 * (i + 1) / DIL_HEADS) for i in range(DIL_HEADS)], np.float32)
    i = np.arange(DIL_QBLK)[:, None]
    j = np.arange(kwin)[None, :]
    tabs = []
    for v in range((kwin - DIL_QBLK) // DIL_HALF + 1):
        diff = np.abs(j - i - v * DIL_HALF)
        dist = (dilation * diff).astype(np.float32)
        bias = -slopes[:, None, None] * dist[None]
        tabs.append(np.where((diff <= DIL_HALF)[None], bias * np.float32(LOG2E), np.float32(NEG)))
    return jnp.asarray(np.stack(tabs).astype(np.float32))


def _dil_call(arr, col0, dilation):
    n, seq, _ = arr.shape
    kwin = min(DIL_KWIN, seq)
    group = max(1, min(n, DIL_STEP_TOKENS // seq))
    assert n % group == 0 and seq % DIL_QBLK == 0
    tab = _dil_tables(dilation, kwin)
    col = lambda off: pl.BlockSpec((group, seq, LANES), lambda j, i, off=off: (i, 0, off + j))
    out_spec = pl.BlockSpec((group, seq, LANES), lambda j, i: (i, 0, j))
    out_sds = jax.ShapeDtypeStruct((n, seq, DIL_WIDTH), jnp.float32)
    return pl.pallas_call(
        functools.partial(_dil_kernel, seq=seq, group=group, kwin=kwin),
        grid=(HEAD_PAIRS, n // group),
        in_specs=[
            col(col0), col(col0 + HEAD_PAIRS), col(col0 + 2 * HEAD_PAIRS),
            pl.BlockSpec((tab.shape[0], 2, DIL_QBLK, kwin), lambda j, i: (0, j, 0, 0)),
        ],
        out_specs=[out_spec, out_spec],
        out_shape=[out_sds, out_sds],
        scratch_shapes=_attn_scratch(min(group * seq // DIL_QBLK, ATTN_RING * DIL_BATCH), DIL_QBLK,
                                     kwin, (group, seq, LANES)),
        compiler_params=pltpu.CompilerParams(
            dimension_semantics=("parallel", "parallel"), vmem_limit_bytes=VMEM_LIMIT),
        name=f"dil_attn_d{dilation}",
    )(arr, arr, arr, tab)


def _tail_kernel(x_ref, oa_ref, o1_ref, l1_ref, o4_ref, l4_ref, o16_ref, l16_ref,
                 gna_ref, gdil_ref, wout_ref, gffn_ref, wg_ref, wu_ref, wd_ref, gfin_ref,
                 y_ref, o4s, l4s, o16s, l16s, tmp_ref, act_ref):
    tm = x_ref.shape[1]
    nslab = DIL_WIDTH // LANES
    r4, r16 = tm // 4, tm // 16
    for src, dst in ((o16_ref, o16s), (l16_ref, l16s)):
        for s in range(nslab):
            lanes = slice(s * LANES, (s + 1) * LANES)
            for w in range(4):
                for u in range(4):
                    tmp_ref[s, pl.ds(w * r4 + u, r16, stride=4), :] = src[0, 4 * w + u, :, lanes]
            for w in range(4):
                dst[s, pl.ds(w, r4, stride=4), :] = tmp_ref[s, w * r4:(w + 1) * r4, :]
    for src, dst in ((o4_ref, o4s), (l4_ref, l4s)):
        for s in range(nslab):
            for w in range(4):
                dst[s, pl.ds(w, r4, stride=4), :] = src[0, w, :, s * LANES:(s + 1) * LANES]

    od_parts = []
    for s in range(nslab):
        sl = slice(s * LANES, (s + 1) * LANES)
        l1, l4, l16 = l1_ref[0, :, sl], l4s[s], l16s[s]
        top = jnp.maximum(jnp.maximum(l1, l4), l16)
        w1, w4, w16 = jnp.exp(l1 - top), jnp.exp(l4 - top), jnp.exp(l16 - top)
        num = w1 * o1_ref[0, :, sl] + w4 * o4s[s] + w16 * o16s[s]
        od_parts.append(num / (w1 + w4 + w16))
    od = jnp.concatenate(od_parts, axis=-1)

    na = _rms(oa_ref[0], gna_ref[...]).astype(jnp.bfloat16)
    dn = _rms(od, gdil_ref[...]).astype(jnp.bfloat16)
    x2 = x_ref[0] + _dot(na, wout_ref[:NA_WIDTH, :]) + _dot(dn, wout_ref[NA_WIDTH:, :])

    hn = _rms(x2, gffn_ref[...]).astype(jnp.bfloat16)
    for c in range(D_FF // FFN_CHUNK):
        sl = slice(c * FFN_CHUNK, (c + 1) * FFN_CHUNK)
        gate = _dot(hn, wg_ref[:, sl])
        up = _dot(hn, wu_ref[:, sl])
        act_ref[:, sl] = (gate * jax.nn.sigmoid(gate) * up).astype(jnp.bfloat16)
    y_ref[0] = _rms(x2 + _dot(act_ref[...], wd_ref[...]), gfin_ref[...])


def _tail_call(x, oa, o1, l1, o4, l4, o16, l16, g_na, g_dil, w_out, g_ffn, w_gate, w_up, w_down,
               g_final):
    b, t, _ = x.shape
    tm = TAIL_TILE
    tok = lambda width: pl.BlockSpec((1, tm, width), lambda i, j: (i, j, 0))
    plane = lambda d: pl.BlockSpec((1, d, tm // d, DIL_WIDTH), lambda i, j: (i, 0, j, 0))
    const = lambda shape: pl.BlockSpec(shape, lambda i, j: (0,) * len(shape),
                                       pipeline_mode=pl.Buffered(1))
    slab = pltpu.VMEM((DIL_WIDTH // LANES, tm, LANES), jnp.float32)
    return pl.pallas_call(
        _tail_kernel,
        grid=(b, t // tm),
        in_specs=[
            tok(D_MODEL), tok(NA_WIDTH), tok(DIL_WIDTH), tok(DIL_WIDTH),
            plane(4), plane(4), plane(16), plane(16),
            const((1, NA_WIDTH)), const((1, DIL_WIDTH)), const((D_MODEL, D_MODEL)),
            const((1, D_MODEL)), const((D_MODEL, D_FF)), const((D_MODEL, D_FF)),
            const((D_FF, D_MODEL)), const((1, D_MODEL)),
        ],
        out_specs=tok(D_MODEL),
        out_shape=jax.ShapeDtypeStruct((b, t, D_MODEL), jnp.float32),
        scratch_shapes=[slab, slab, slab, slab, slab, pltpu.VMEM((tm, D_FF), jnp.bfloat16)],
        compiler_params=pltpu.CompilerParams(
            dimension_semantics=("parallel", "parallel"), vmem_limit_bytes=VMEM_LIMIT),
        name="tail",
    )(x, oa, o1, l1, o4, l4, o16, l16, g_na, g_dil, w_out, g_ffn, w_gate, w_up, w_down, g_final)


def _na_table(rpb):
    qc = np.arange(GRID_W)[:, None]
    kc = np.arange(GRID_W)[None, :]
    wstart = np.clip(qc - NA_COLS // 2, 0, GRID_W - NA_COLS)
    ok = (kc >= wstart) & (kc < wstart + NA_COLS)
    relc = np.clip(kc - qc + NA_COLS - 1, 0, 2 * NA_COLS - 2)
    onehot = (relc[None] == np.arange(2 * NA_COLS - 1)[:, None, None]).astype(np.float32)
    colbias = jnp.einsum("hrc,cqk->hrqk", rpb.astype(jnp.float32), jnp.asarray(onehot),
                         precision=lax.Precision.HIGHEST)
    colbias = jnp.where(jnp.asarray(ok)[None, None], colbias * LOG2E, NEG)
    variants = []
    for a in range(NA_ROWS):
        win = colbias[:, NA_ROWS - 1 - a:2 * NA_ROWS - 1 - a]
        variants.append(win.transpose(0, 2, 1, 3).reshape(NA_HEADS, GRID_W, NA_ROWS * GRID_W))
    return jnp.stack(variants, axis=1)


def _trunk(x, w_in, na_tab, g_attn, g_na, g_dil, w_out, g_ffn, w_gate, w_up, w_down, g_final):
    b, t, _ = x.shape
    nat, d4, d16 = _qkv_call(x, g_attn, w_in)
    oa = _na_call(nat, na_tab)
    o1, l1 = _dil_call(nat, (NA_WIDTH * 3) // LANES, 1)
    o4, l4 = _dil_call(d4.reshape(b * 4, t // 4, DIL_QKV), 0, 4)
    o16, l16 = _dil_call(d16.reshape(b * 16, t // 16, DIL_QKV), 0, 16)
    plane = lambda z, d: z.reshape(b, d, t // d, DIL_WIDTH)
    return _tail_call(x, oa, o1, l1, plane(o4, 4), plane(l4, 4), plane(o16, 16), plane(l16, 16),
                      g_na, g_dil, w_out, g_ffn, w_gate, w_up, w_down, g_final)


def kernel(x_prompt, x_sample, w_in, rpb, g_attn, g_na, g_dil, w_out, g_ffn, w_gate, w_up, w_down,
           g_final):
    assert w_in.shape[0] == 1, "single-layer trunk"
    bf = lambda w: w[0].astype(jnp.bfloat16)
    params = (bf(w_in), _na_table(rpb[0]), g_attn, g_na, g_dil, bf(w_out), g_ffn, bf(w_gate), bf(w_up),
              bf(w_down), g_final.reshape(1, D_MODEL))
    return _trunk(x_prompt, *params), _trunk(x_sample, *params)
```

```python
import functools

import jax
import jax.numpy as jnp
import numpy as np
from jax import lax
from jax.experimental import pallas as pl
from jax.experimental.pallas import tpu as pltpu

D_MODEL = 1024
HEAD_DIM = 64
NA_HEADS = 8
DIL_HEADS = 8
NA_WIDTH = NA_HEADS * HEAD_DIM
DIL_WIDTH = DIL_HEADS * HEAD_DIM
QKV_WIDTH = 3 * (NA_WIDTH + DIL_WIDTH)
DIL_QKV = 3 * DIL_WIDTH
D_FF = 2816
GRID_W = 64
NA_ROWS = 8
NA_COLS = 16
DIL_PATTERNS = ((128, 1), (512, 4), (2048, 16))
DIL_HALF = 64
RMS_EPS = 1e-6
NEG = -1e30
LOG2E = 1.4426950408889634
LN2 = 0.6931471805599453
Q_SCALE = HEAD_DIM ** -0.5 * LOG2E

LANES = 128
HEAD_PAIRS = NA_HEADS // 2
QKV_TILE = 512
TAIL_TILE = 256
DIL_QBLK = 128
DIL_KWIN = 256
DIL_STEP_TOKENS = 4096
FFN_CHUNK = 256
QKV_CHUNK = 256
NA_STEP_ROWS = 64
NA_BATCH = 4
DIL_BATCH = 4
ATTN_RING = 2
VMEM_LIMIT = 56 * 1024 * 1024

assert all(w // (2 * d) == DIL_HALF for w, d in DIL_PATTERNS)


def _rms(x, g):
    return x * lax.rsqrt(jnp.mean(x * x, axis=-1, keepdims=True) + RMS_EPS) * g


def _dot_nt(a, b):
    return lax.dot_general(a, b, (((1,), (1,)), ((), ())), preferred_element_type=jnp.float32)


def _dot(a, b):
    return jnp.dot(a, b, preferred_element_type=jnp.float32)


def _qkv_kernel(x_ref, g_ref, w_ref, nat_ref, d4_ref, d16_ref, slab_ref, slab4_ref, hn_ref):
    tm = x_ref.shape[1]

    hn_ref[...] = _rms(x_ref[0], g_ref[...]).astype(jnp.bfloat16)
    chunk = QKV_CHUNK
    slabs = chunk // LANES
    nchunk = QKV_WIDTH // chunk
    per_role = NA_WIDTH // chunk
    for c in list(range(nchunk // 2, nchunk)) + list(range(nchunk // 2)):
        acc = _dot(hn_ref[...], w_ref[:, c * chunk:(c + 1) * chunk])
        if (c // per_role) % 3 == 0:
            acc = acc * Q_SCALE
        nat_ref[0, :, c * chunk:(c + 1) * chunk] = acc.astype(jnp.bfloat16)
        if c < nchunk // 2:
            continue
        s0 = (c - nchunk // 2) * slabs
        for j in range(slabs):
            slab_ref[s0 + j] = acc[:, j * LANES:(j + 1) * LANES]
        r4, r16 = tm // 4, tm // 16
        for s in range(s0, s0 + slabs):
            lanes = slice(s * LANES, (s + 1) * LANES)
            for w in range(4):
                plane = slab_ref[s, pl.ds(w, r4, stride=4), :]
                d4_ref[0, w, :, lanes] = plane.astype(jnp.bfloat16)
                slab4_ref[s, w * r4:(w + 1) * r4, :] = plane
            for w in range(4):
                for u in range(4):
                    d16_ref[0, 4 * w + u, :, lanes] = (
                        slab4_ref[s, pl.ds(w * r4 + u, r16, stride=4), :].astype(jnp.bfloat16))


def _qkv_call(x, g, w):
    b, t, _ = x.shape
    tm = QKV_TILE
    return pl.pallas_call(
        _qkv_kernel,
        grid=(b, t // tm),
        in_specs=[
            pl.BlockSpec((1, tm, D_MODEL), lambda i, j: (i, j, 0)),
            pl.BlockSpec((1, D_MODEL), lambda i, j: (0, 0)),
            pl.BlockSpec((D_MODEL, QKV_WIDTH), lambda i, j: (0, 0), pipeline_mode=pl.Buffered(1)),
        ],
        out_specs=[
            pl.BlockSpec((1, tm, QKV_WIDTH), lambda i, j: (i, j, 0)),
            pl.BlockSpec((1, 4, tm // 4, DIL_QKV), lambda i, j: (i, 0, j, 0)),
            pl.BlockSpec((1, 16, tm // 16, DIL_QKV), lambda i, j: (i, 0, j, 0)),
        ],
        out_shape=[
            jax.ShapeDtypeStruct((b, t, QKV_WIDTH), jnp.bfloat16),
            jax.ShapeDtypeStruct((b, 4, t // 4, DIL_QKV), jnp.bfloat16),
            jax.ShapeDtypeStruct((b, 16, t // 16, DIL_QKV), jnp.bfloat16),
        ],
        scratch_shapes=[pltpu.VMEM((DIL_QKV // LANES, tm, LANES), jnp.float32),
                        pltpu.VMEM((DIL_QKV // LANES, tm, LANES), jnp.float32),
                        pltpu.VMEM((tm, D_MODEL), jnp.bfloat16)],
        compiler_params=pltpu.CompilerParams(
            dimension_semantics=("parallel", "parallel"), vmem_limit_bytes=VMEM_LIMIT),
        name="qkv_proj",
    )(x, g, w)


def _low_lanes(shape):
    return lax.broadcasted_iota(jnp.int32, shape, len(shape) - 1) < HEAD_DIM


def _fill_value_planes(v, vaug_ref):
    low = _low_lanes(v.shape)
    one = jnp.ones_like(v)
    vaug_ref[0] = jnp.where(low, v, one)
    vaug_ref[1] = jnp.where(low, one, v)


def _staged_attention(nblk, batch, stacked_scores, geom, load_q, load_k, load_v, bias, emit,
                      s_ref, mx_ref):
    m = s_ref.shape[1]
    kwin = s_ref.shape[2]
    low = _low_lanes((m, LANES))
    ring = s_ref.shape[0] // 2
    assert ring >= 2 * batch or ring >= nblk

    def scores(i):
        qidx, kidx, var = geom(i)
        qb = load_q(qidx)
        zero = jnp.zeros_like(qb)
        kb = load_k(kidx)
        qh = (jnp.where(low, qb, zero), jnp.where(low, zero, qb))
        if stacked_scores:
            s2 = _dot_nt(jnp.concatenate(qh, axis=0), kb)
            sh = (s2[:m], s2[m:])
        else:
            sh = (_dot_nt(qh[0], kb), _dot_nt(qh[1], kb))
        for h in range(2):
            u = 2 * (i % ring) + h
            s = sh[h] + bias(var, h)
            s_ref[u] = s
            mx_ref[u] = jnp.broadcast_to(jnp.max(s, axis=-1, keepdims=True), (m, LANES))

    def values(i):
        qidx, kidx, _ = geom(i)
        u0 = 2 * (i % ring)
        pv = []
        for h in range(2):
            mx = mx_ref[u0 + h]
            e = jnp.concatenate(
                [jnp.exp2(s_ref[u0 + h, :, c * LANES:(c + 1) * LANES] - mx)
                 for c in range(kwin // LANES)], axis=1)
            pv.append(_dot(e.astype(jnp.bfloat16), load_v(h, kidx)))
        num = jnp.where(low, pv[0], pv[1])
        den = pltpu.roll(jnp.where(low, pv[1], pv[0]), HEAD_DIM, 1)
        emit(qidx, num / den, jnp.where(low, mx_ref[u0], mx_ref[u0 + 1]), den)

    assert nblk % batch == 0
    nbatch = nblk // batch
    for b in range(nbatch + 1):
        for i in range(batch):
            if b < nbatch:
                scores(b * batch + i)
            if b >= 1:
                values((b - 1) * batch + i)


def _attn_scratch(ring, m, kwin, vaug_shape):
    return [
        pltpu.VMEM((2 * ring, m, kwin), jnp.float32),
        pltpu.VMEM((2 * ring, m, LANES), jnp.float32),
        pltpu.VMEM((2,) + vaug_shape, jnp.bfloat16),
    ]


def _na_kernel(q_ref, k_ref, v_ref, tab_ref, o_ref, s_ref, mx_ref, vaug_ref, *, rows):
    _fill_value_planes(v_ref[...], vaug_ref)

    def geom(i):
        g, r = i // rows, i % rows
        rs = min(max(r - NA_ROWS // 2, 0), rows - NA_ROWS)
        return (g, r * GRID_W), (g, rs * GRID_W), r - rs

    def emit(qidx, out, mx, den):
        o_ref[qidx[0], pl.ds(qidx[1], GRID_W), :] = out

    _staged_attention(
        q_ref.shape[0] * rows, NA_BATCH, True, geom,
        load_q=lambda qidx: q_ref[qidx[0], pl.ds(qidx[1], GRID_W), :],
        load_k=lambda kidx: k_ref[kidx[0], pl.ds(kidx[1], NA_ROWS * GRID_W), :],
        load_v=lambda h, kidx: vaug_ref[h, kidx[0], pl.ds(kidx[1], NA_ROWS * GRID_W), :],
        bias=lambda var, h: tab_ref[h, var],
        emit=emit, s_ref=s_ref, mx_ref=mx_ref)


def _na_call(nat, tab):
    b, t, _ = nat.shape
    rows = t // GRID_W
    kw = NA_ROWS * GRID_W
    group = max(1, min(b, NA_STEP_ROWS // rows))
    assert b % group == 0
    col = lambda off: pl.BlockSpec((group, t, LANES), lambda j, i, off=off: (i, 0, off + j))
    return pl.pallas_call(
        functools.partial(_na_kernel, rows=rows),
        grid=(HEAD_PAIRS, b // group),
        in_specs=[
            col(0), col(HEAD_PAIRS), col(2 * HEAD_PAIRS),
            pl.BlockSpec((2, NA_ROWS, GRID_W, kw), lambda j, i: (j, 0, 0, 0)),
        ],
        out_specs=pl.BlockSpec((group, t, LANES), lambda j, i: (i, 0, j)),
        out_shape=jax.ShapeDtypeStruct((b, t, NA_WIDTH), jnp.float32),
        scratch_shapes=_attn_scratch(min(group * rows, ATTN_RING * NA_BATCH), GRID_W, kw,
                                     (group, t, LANES)),
        compiler_params=pltpu.CompilerParams(
            dimension_semantics=("parallel", "parallel"), vmem_limit_bytes=VMEM_LIMIT),
        name="na_attn",
    )(nat, nat, nat, tab)


def _dil_kernel(q_ref, k_ref, v_ref, tab_ref, o_ref, lse_ref, s_ref, mx_ref, vaug_ref,
                *, seq, group, kwin):
    nblk = seq // DIL_QBLK
    _fill_value_planes(v_ref[...], vaug_ref)

    def geom(i):
        g, q0 = i // nblk, (i % nblk) * DIL_QBLK
        k0 = min(max(q0 - DIL_HALF, 0), seq - kwin)
        return (g, q0), (g, k0), (q0 - k0) // DIL_HALF

    def emit(qidx, out, mx, den):
        g, q0 = qidx
        o_ref[g, pl.ds(q0, DIL_QBLK), :] = out
        lse_ref[g, pl.ds(q0, DIL_QBLK), :] = mx * LN2 + jnp.log(den)

    _staged_attention(
        group * nblk, DIL_BATCH, False, geom,
        load_q=lambda qidx: q_ref[qidx[0], pl.ds(qidx[1], DIL_QBLK), :],
        load_k=lambda kidx: k_ref[kidx[0], pl.ds(kidx[1], kwin), :],
        load_v=lambda h, kidx: vaug_ref[h, kidx[0], pl.ds(kidx[1], kwin), :],
        bias=lambda var, h: tab_ref[var, h],
        emit=emit, s_ref=s_ref, mx_ref=mx_ref)


def _dil_tables(dilation, kwin):
    slopes = np.array([2.0 ** (-8.0 * (i + 1) / DIL_HEADS) for i in range(DIL_HEADS)], np.float32)
    i = np.arange(DIL_QBLK)[:, None]
    j = np.arange(kwin)[None, :]
    tabs = []
    for v in range((kwin - DIL_QBLK) // DIL_HALF + 1):
        diff = np.abs(j - i - v * DIL_HALF)
        dist = (dilation * diff).astype(np.float32)
        bias = -slopes[:, None, None] * dist[None]
        tabs.append(np.where((diff <= DIL_HALF)[None], bias * np.float32(LOG2E), np.float32(NEG)))
    return jnp.asarray(np.stack(tabs).astype(np.float32))


def _dil_call(arr, col0, dilation):
    n, seq, _ = arr.shape
    kwin = min(DIL_KWIN, seq)
    group = max(1, min(n, DIL_STEP_TOKENS // seq))
    assert n % group == 0 and seq % DIL_QBLK == 0
    tab = _dil_tables(dilation, kwin)
    col = lambda off: pl.BlockSpec((group, seq, LANES), lambda j, i, off=off: (i, 0, off + j))
    out_spec = pl.BlockSpec((group, seq, LANES), lambda j, i: (i, 0, j))
    out_sds = jax.ShapeDtypeStruct((n, seq, DIL_WIDTH), jnp.float32)
    return pl.pallas_call(
        functools.partial(_dil_kernel, seq=seq, group=group, kwin=kwin),
        grid=(HEAD_PAIRS, n // group),
        in_specs=[
            col(col0), col(col0 + HEAD_PAIRS), col(col0 + 2 * HEAD_PAIRS),
            pl.BlockSpec((tab.shape[0], 2, DIL_QBLK, kwin), lambda j, i: (0, j, 0, 0)),
        ],
        out_specs=[out_spec, out_spec],
        out_shape=[out_sds, out_sds],
        scratch_shapes=_attn_scratch(min(group * seq // DIL_QBLK, ATTN_RING * DIL_BATCH), DIL_QBLK,
                                     kwin, (group, seq, LANES)),
        compiler_params=pltpu.CompilerParams(
            dimension_semantics=("parallel", "parallel"), vmem_limit_bytes=VMEM_LIMIT),
        name=f"dil_attn_d{dilation}",
    )(arr, arr, arr, tab)


def _tail_kernel(x_ref, oa_ref, o1_ref, l1_ref, o4_ref, l4_ref, o16_ref, l16_ref,
                 gna_ref, gdil_ref, wout_ref, gffn_ref, wg_ref, wu_ref, wd_ref, gfin_ref,
                 y_ref, o4s, l4s, o16s, l16s, tmp_ref, act_ref, x2_ref, hn_ref, od_ref):
    step = pl.program_id(0)

    @pl.when(step == 0)
    def _():
        x2_ref[1] = jnp.zeros(x2_ref.shape[1:], x2_ref.dtype)
        hn_ref[1] = jnp.zeros(hn_ref.shape[1:], hn_ref.dtype)

    def body(read, write):
        mix = _tail_mix(x_ref, oa_ref, o1_ref, l1_ref, o4_ref, l4_ref, o16_ref, l16_ref,
                        gna_ref, gdil_ref, wout_ref, gffn_ref, o4s, l4s, o16s, l16s, tmp_ref,
                        od_ref, x2_ref.at[write], hn_ref.at[write])
        hn = hn_ref[read]
        for c in range(D_FF // FFN_CHUNK):
            sl = slice(c * FFN_CHUNK, (c + 1) * FFN_CHUNK)
            gate = _dot(hn, wg_ref[:, sl])
            up = _dot(hn, wu_ref[:, sl])
            act_ref[:, sl] = (gate * jax.nn.sigmoid(gate) * up).astype(jnp.bfloat16)
            next(mix, None)
        for _ in mix:
            pass
        y_ref[0] = _rms(x2_ref[read] + _dot(act_ref[...], wd_ref[...]), gfin_ref[...])

    parity = lax.rem(step, 2)
    pl.when(parity == 0)(lambda: body(1, 0))
    pl.when(parity == 1)(lambda: body(0, 1))


def _tail_mix(x_ref, oa_ref, o1_ref, l1_ref, o4_ref, l4_ref, o16_ref, l16_ref,
              gna_ref, gdil_ref, wout_ref, gffn_ref, o4s, l4s, o16s, l16s, tmp_ref,
              od_ref, x2_out, hn_out):
    tm = x_ref.shape[1]
    nslab = DIL_WIDTH // LANES
    r4, r16 = tm // 4, tm // 16
    for s in range(nslab):
        lanes = slice(s * LANES, (s + 1) * LANES)
        for src, dst in ((o16_ref, o16s), (l16_ref, l16s)):
            for w in range(4):
                for u in range(4):
                    tmp_ref[s, pl.ds(w * r4 + u, r16, stride=4), :] = src[0, 4 * w + u, :, lanes]
            for w in range(4):
                dst[s, pl.ds(w, r4, stride=4), :] = tmp_ref[s, w * r4:(w + 1) * r4, :]
        for src, dst in ((o4_ref, o4s), (l4_ref, l4s)):
            for w in range(4):
                dst[s, pl.ds(w, r4, stride=4), :] = src[0, w, :, lanes]
        l1, l4, l16 = l1_ref[0, :, lanes], l4s[s], l16s[s]
        top = jnp.maximum(jnp.maximum(l1, l4), l16)
        w1, w4, w16 = jnp.exp(l1 - top), jnp.exp(l4 - top), jnp.exp(l16 - top)
        num = w1 * o1_ref[0, :, lanes] + w4 * o4s[s] + w16 * o16s[s]
        od_ref[:, lanes] = num / (w1 + w4 + w16)
        yield

    na = _rms(oa_ref[0], gna_ref[...]).astype(jnp.bfloat16)
    dn = _rms(od_ref[...], gdil_ref[...]).astype(jnp.bfloat16)
    x2_out[...] = x_ref[0] + _dot(na, wout_ref[:NA_WIDTH, :]) + _dot(dn, wout_ref[NA_WIDTH:, :])
    yield
    hn_out[...] = _rms(x2_out[...], gffn_ref[...]).astype(jnp.bfloat16)
    yield


def _tail_call(x, oa, o1, l1, o4, l4, o16, l16, g_na, g_dil, w_out, g_ffn, w_gate, w_up, w_down,
               g_final):
    b, t, _ = x.shape
    tm = TAIL_TILE
    nt = t // tm
    last = b * nt - 1

    def mixed(g):
        p = jnp.minimum(g, last)
        return p // nt, p % nt

    def fed(g):
        f = jnp.maximum(g - 1, 0)
        return f // nt, f % nt

    tok = lambda width: pl.BlockSpec((1, tm, width), lambda g: mixed(g) + (0,))
    plane = lambda d: pl.BlockSpec(
        (1, d, tm // d, DIL_WIDTH), lambda g: (mixed(g)[0], 0, mixed(g)[1], 0))
    const = lambda shape: pl.BlockSpec(shape, lambda g: (0,) * len(shape),
                                       pipeline_mode=pl.Buffered(1))
    slab = pltpu.VMEM((DIL_WIDTH // LANES, tm, LANES), jnp.float32)
    return pl.pallas_call(
        _tail_kernel,
        grid=(b * nt + 1,),
        in_specs=[
            tok(D_MODEL), tok(NA_WIDTH), tok(DIL_WIDTH), tok(DIL_WIDTH),
            plane(4), plane(4), plane(16), plane(16),
            const((1, NA_WIDTH)), const((1, DIL_WIDTH)), const((D_MODEL, D_MODEL)),
            const((1, D_MODEL)), const((D_MODEL, D_FF)), const((D_MODEL, D_FF)),
            const((D_FF, D_MODEL)), const((1, D_MODEL)),
        ],
        out_specs=pl.BlockSpec((1, tm, D_MODEL), lambda g: fed(g) + (0,)),
        out_shape=jax.ShapeDtypeStruct((b, t, D_MODEL), jnp.float32),
        scratch_shapes=[slab, slab, slab, slab, slab, pltpu.VMEM((tm, D_FF), jnp.bfloat16),
                        pltpu.VMEM((2, tm, D_MODEL), jnp.float32),
                        pltpu.VMEM((2, tm, D_MODEL), jnp.bfloat16),
                        pltpu.VMEM((tm, DIL_WIDTH), jnp.float32)],
        compiler_params=pltpu.CompilerParams(
            dimension_semantics=("arbitrary",), vmem_limit_bytes=VMEM_LIMIT),
        name="tail",
    )(x, oa, o1, l1, o4, l4, o16, l16, g_na, g_dil, w_out, g_ffn, w_gate, w_up, w_down, g_final)


def _na_table(rpb):
    qc = np.arange(GRID_W)[:, None]
    kc = np.arange(GRID_W)[None, :]
    wstart = np.clip(qc - NA_COLS // 2, 0, GRID_W - NA_COLS)
    ok = (kc >= wstart) & (kc < wstart + NA_COLS)
    relc = np.clip(kc - qc + NA_COLS - 1, 0, 2 * NA_COLS - 2)
    onehot = (relc[None] == np.arange(2 * NA_COLS - 1)[:, None, None]).astype(np.float32)
    colbias = jnp.einsum("hrc,cqk->hrqk", rpb.astype(jnp.float32), jnp.asarray(onehot),
                         precision=lax.Precision.HIGHEST)
    colbias = jnp.where(jnp.asarray(ok)[None, None], colbias * LOG2E, NEG)
    variants = []
    for a in range(NA_ROWS):
        win = colbias[:, NA_ROWS - 1 - a:2 * NA_ROWS - 1 - a]
        variants.append(win.transpose(0, 2, 1, 3).reshape(NA_HEADS, GRID_W, NA_ROWS * GRID_W))
    return jnp.stack(variants, axis=1)


def _trunk(x, w_in, na_tab, g_attn, g_na, g_dil, w_out, g_ffn, w_gate, w_up, w_down, g_final):
    b, t, _ = x.shape
    nat, d4, d16 = _qkv_call(x, g_attn, w_in)
    oa = _na_call(nat, na_tab)
    o1, l1 = _dil_call(nat, (NA_WIDTH * 3) // LANES, 1)
    o4, l4 = _dil_call(d4.reshape(b * 4, t // 4, DIL_QKV), 0, 4)
    o16, l16 = _dil_call(d16.reshape(b * 16, t // 16, DIL_QKV), 0, 16)
    plane = lambda z, d: z.reshape(b, d, t // d, DIL_WIDTH)
    return _tail_call(x, oa, o1, l1, plane(o4, 4), plane(l4, 4), plane(o16, 16), plane(l16, 16),
                      g_na, g_dil, w_out, g_ffn, w_gate, w_up, w_down, g_final)


def kernel(x_prompt, x_sample, w_in, rpb, g_attn, g_na, g_dil, w_out, g_ffn, w_gate, w_up, w_down,
           g_final):
    assert w_in.shape[0] == 1, "single-layer trunk"
    bf = lambda w: w[0].astype(jnp.bfloat16)
    params = (bf(w_in), _na_table(rpb[0]), g_attn, g_na, g_dil, bf(w_out), g_ffn, bf(w_gate), bf(w_up),
              bf(w_down), g_final.reshape(1, D_MODEL))
    return _trunk(x_prompt, *params), _trunk(x_sample, *params)
```

```python
import functools

import jax
import jax.numpy as jnp
import numpy as np
from jax import lax
from jax.experimental import pallas as pl
from jax.experimental.pallas import tpu as pltpu

D_MODEL = 1024
HEAD_DIM = 64
NA_HEADS = 8
DIL_HEADS = 8
NA_WIDTH = NA_HEADS * HEAD_DIM
DIL_WIDTH = DIL_HEADS * HEAD_DIM
QKV_WIDTH = 3 * (NA_WIDTH + DIL_WIDTH)
DIL_QKV = 3 * DIL_WIDTH
D_FF = 2816
GRID_W = 64
NA_ROWS = 8
NA_COLS = 16
DIL_PATTERNS = ((128, 1), (512, 4), (2048, 16))
DIL_HALF = 64
RMS_EPS = 1e-6
NEG = -1e30
LOG2E = 1.4426950408889634
LN2 = 0.6931471805599453
Q_SCALE = HEAD_DIM ** -0.5 * LOG2E

LANES = 128
HEAD_PAIRS = NA_HEADS // 2
QKV_TILE = 512
TAIL_TILE = 512
DIL_QBLK = 128
DIL_KWIN = 256
DIL_STEP_TOKENS = 4096
FFN_CHUNK = 256
QKV_CHUNK = 256
NA_STEP_ROWS = 64
NA_BATCH = 4
DIL_BATCH = 4
ATTN_RING = 2
VMEM_LIMIT = 56 * 1024 * 1024

assert all(w // (2 * d) == DIL_HALF for w, d in DIL_PATTERNS)


def _rms(x, g):
    return x * lax.rsqrt(jnp.mean(x * x, axis=-1, keepdims=True) + RMS_EPS) * g


def _dot_nt(a, b):
    return lax.dot_general(a, b, (((1,), (1,)), ((), ())), preferred_element_type=jnp.float32)


def _dot(a, b):
    return jnp.dot(a, b, preferred_element_type=jnp.float32)


def _qkv_kernel(x_ref, g_ref, w_ref, nat_ref, d4_ref, d16_ref, slab_ref, slab4_ref, hn_ref):
    tm = x_ref.shape[1]
    hn_ref[...] = _rms(x_ref[0], g_ref[...]).astype(jnp.bfloat16)
    chunk = QKV_CHUNK
    slabs = chunk // LANES
    nchunk = QKV_WIDTH // chunk
    per_role = NA_WIDTH // chunk
    for c in list(range(nchunk // 2, nchunk)) + list(range(nchunk // 2)):
        acc = _dot(hn_ref[...], w_ref[:, c * chunk:(c + 1) * chunk])
        if (c // per_role) % 3 == 0:
            acc = acc * Q_SCALE
        nat_ref[0, :, c * chunk:(c + 1) * chunk] = acc.astype(jnp.bfloat16)
        if c < nchunk // 2:
            continue
        s0 = (c - nchunk // 2) * slabs
        for j in range(slabs):
            slab_ref[s0 + j] = acc[:, j * LANES:(j + 1) * LANES]
        r4, r16 = tm // 4, tm // 16
        for s in range(s0, s0 + slabs):
            lanes = slice(s * LANES, (s + 1) * LANES)
            for w in range(4):
                plane = slab_ref[s, pl.ds(w, r4, stride=4), :]
                d4_ref[0, w, :, lanes] = plane.astype(jnp.bfloat16)
                slab4_ref[s, w * r4:(w + 1) * r4, :] = plane
            for w in range(4):
                for u in range(4):
                    d16_ref[0, 4 * w + u, :, lanes] = (
                        slab4_ref[s, pl.ds(w * r4 + u, r16, stride=4), :].astype(jnp.bfloat16))


def _qkv_call(x, g, w):
    b, t, _ = x.shape
    tm = QKV_TILE
    return pl.pallas_call(
        _qkv_kernel,
        grid=(b, t // tm),
        in_specs=[
            pl.BlockSpec((1, tm, D_MODEL), lambda i, j: (i, j, 0)),
            pl.BlockSpec((1, D_MODEL), lambda i, j: (0, 0)),
            pl.BlockSpec((D_MODEL, QKV_WIDTH), lambda i, j: (0, 0), pipeline_mode=pl.Buffered(1)),
        ],
        out_specs=[
            pl.BlockSpec((1, tm, QKV_WIDTH), lambda i, j: (i, j, 0)),
            pl.BlockSpec((1, 4, tm // 4, DIL_QKV), lambda i, j: (i, 0, j, 0)),
            pl.BlockSpec((1, 16, tm // 16, DIL_QKV), lambda i, j: (i, 0, j, 0)),
        ],
        out_shape=[
            jax.ShapeDtypeStruct((b, t, QKV_WIDTH), jnp.bfloat16),
            jax.ShapeDtypeStruct((b, 4, t // 4, DIL_QKV), jnp.bfloat16),
            jax.ShapeDtypeStruct((b, 16, t // 16, DIL_QKV), jnp.bfloat16),
        ],
        scratch_shapes=[pltpu.VMEM((DIL_QKV // LANES, tm, LANES), jnp.float32),
                        pltpu.VMEM((DIL_QKV // LANES, tm, LANES), jnp.float32),
                        pltpu.VMEM((tm, D_MODEL), jnp.bfloat16)],
        compiler_params=pltpu.CompilerParams(
            dimension_semantics=("parallel", "parallel"), vmem_limit_bytes=VMEM_LIMIT),
        name="qkv_proj",
    )(x, g, w)


def _low_lanes(shape):
    return lax.broadcasted_iota(jnp.int32, shape, len(shape) - 1) < HEAD_DIM


def _fill_value_planes(v, vaug_ref):
    low = _low_lanes(v.shape)
    one = jnp.ones_like(v)
    vaug_ref[0] = jnp.where(low, v, one)
    vaug_ref[1] = jnp.where(low, one, v)


def _staged_attention(nblk, batch, stacked_scores, geom, load_q, load_k, load_v, bias, emit,
                      s_ref, mx_ref):
    m = s_ref.shape[1]
    kwin = s_ref.shape[2]
    low = _low_lanes((m, LANES))
    ring = s_ref.shape[0] // 2
    assert ring >= 2 * batch or ring >= nblk

    def scores(i):
        qidx, kidx, var = geom(i)
        qb = load_q(qidx)
        zero = jnp.zeros_like(qb)
        kb = load_k(kidx)
        qh = (jnp.where(low, qb, zero), jnp.where(low, zero, qb))
        if stacked_scores:
            s2 = _dot_nt(jnp.concatenate(qh, axis=0), kb)
            sh = (s2[:m], s2[m:])
        else:
            sh = (_dot_nt(qh[0], kb), _dot_nt(qh[1], kb))
        for h in range(2):
            u = 2 * (i % ring) + h
            s = sh[h] + bias(var, h)
            s_ref[u] = s
            mx_ref[u] = jnp.broadcast_to(jnp.max(s, axis=-1, keepdims=True), (m, LANES))

    def values(i):
        qidx, kidx, _ = geom(i)
        u0 = 2 * (i % ring)
        pv = []
        for h in range(2):
            mx = mx_ref[u0 + h]
            e = jnp.concatenate(
                [jnp.exp2(s_ref[u0 + h, :, c * LANES:(c + 1) * LANES] - mx)
                 for c in range(kwin // LANES)], axis=1)
            pv.append(_dot(e.astype(jnp.bfloat16), load_v(h, kidx)))
        num = jnp.where(low, pv[0], pv[1])
        den = pltpu.roll(jnp.where(low, pv[1], pv[0]), HEAD_DIM, 1)
        emit(qidx, num / den, jnp.where(low, mx_ref[u0], mx_ref[u0 + 1]), den)

    assert nblk % batch == 0
    nbatch = nblk // batch
    for b in range(nbatch + 1):
        for i in range(batch):
            if b < nbatch:
                scores(b * batch + i)
            if b >= 1:
                values((b - 1) * batch + i)


def _attn_scratch(ring, m, kwin, vaug_shape):
    return [
        pltpu.VMEM((2 * ring, m, kwin), jnp.float32),
        pltpu.VMEM((2 * ring, m, LANES), jnp.float32),
        pltpu.VMEM((2,) + vaug_shape, jnp.bfloat16),
    ]


def _na_kernel(q_ref, k_ref, v_ref, tab_ref, o_ref, s_ref, mx_ref, vaug_ref, *, rows):
    _fill_value_planes(v_ref[...], vaug_ref)

    def geom(i):
        g, r = i // rows, i % rows
        rs = min(max(r - NA_ROWS // 2, 0), rows - NA_ROWS)
        return (g, r * GRID_W), (g, rs * GRID_W), r - rs

    def emit(qidx, out, mx, den):
        o_ref[qidx[0], pl.ds(qidx[1], GRID_W), :] = out

    _staged_attention(
        q_ref.shape[0] * rows, NA_BATCH, True, geom,
        load_q=lambda qidx: q_ref[qidx[0], pl.ds(qidx[1], GRID_W), :],
        load_k=lambda kidx: k_ref[kidx[0], pl.ds(kidx[1], NA_ROWS * GRID_W), :],
        load_v=lambda h, kidx: vaug_ref[h, kidx[0], pl.ds(kidx[1], NA_ROWS * GRID_W), :],
        bias=lambda var, h: tab_ref[h, var],
        emit=emit, s_ref=s_ref, mx_ref=mx_ref)


def _na_call(nat, tab):
    b, t, _ = nat.shape
    rows = t // GRID_W
    kw = NA_ROWS * GRID_W
    group = max(1, min(b, NA_STEP_ROWS // rows))
    assert b % group == 0
    col = lambda off: pl.BlockSpec((group, t, LANES), lambda j, i, off=off: (i, 0, off + j))
    return pl.pallas_call(
        functools.partial(_na_kernel, rows=rows),
        grid=(HEAD_PAIRS, b // group),
        in_specs=[
            col(0), col(HEAD_PAIRS), col(2 * HEAD_PAIRS),
            pl.BlockSpec((2, NA_ROWS, GRID_W, kw), lambda j, i: (j, 0, 0, 0)),
        ],
        out_specs=pl.BlockSpec((group, t, LANES), lambda j, i: (i, 0, j)),
        out_shape=jax.ShapeDtypeStruct((b, t, NA_WIDTH), jnp.float32),
        scratch_shapes=_attn_scratch(min(group * rows, ATTN_RING * NA_BATCH), GRID_W, kw,
                                     (group, t, LANES)),
        compiler_params=pltpu.CompilerParams(
            dimension_semantics=("parallel", "parallel"), vmem_limit_bytes=VMEM_LIMIT),
        name="na_attn",
    )(nat, nat, nat, tab)


def _dil_kernel(q_ref, k_ref, v_ref, tab_ref, o_ref, lse_ref, s_ref, mx_ref, vaug_ref,
                *, seq, group, kwin):
    nblk = seq // DIL_QBLK
    _fill_value_planes(v_ref[...], vaug_ref)

    def geom(i):
        g, q0 = i // nblk, (i % nblk) * DIL_QBLK
        k0 = min(max(q0 - DIL_HALF, 0), seq - kwin)
        return (g, q0), (g, k0), (q0 - k0) // DIL_HALF

    def emit(qidx, out, mx, den):
        g, q0 = qidx
        o_ref[g, pl.ds(q0, DIL_QBLK), :] = out
        lse_ref[g, pl.ds(q0, DIL_QBLK), :] = mx * LN2 + jnp.log(den)

    _staged_attention(
        group * nblk, DIL_BATCH, False, geom,
        load_q=lambda qidx: q_ref[qidx[0], pl.ds(qidx[1], DIL_QBLK), :],
        load_k=lambda kidx: k_ref[kidx[0], pl.ds(kidx[1], kwin), :],
        load_v=lambda h, kidx: vaug_ref[h, kidx[0], pl.ds(kidx[1], kwin), :],
        bias=lambda var, h: tab_ref[var, h],
        emit=emit, s_ref=s_ref, mx_ref=mx_ref)


def _dil_tables(dilation, kwin):
    slopes = np.array([2.0 ** (-8.0 * (i + 1) / DIL_HEADS) for i in range(DIL_HEADS)], np.float32)
    i = np.arange(DIL_QBLK)[:, None]
    j = np.arange(kwin)[None, :]
    tabs = []
    for v in range((kwin - DIL_QBLK) // DIL_HALF + 1):
        diff = np.abs(j - i - v * DIL_HALF)
        dist = (dilation * diff).astype(np.float32)
        bias = -slopes[:, None, None] * dist[None]
        tabs.append(np.where((diff <= DIL_HALF)[None], bias * np.float32(LOG2E), np.float32(NEG)))
    return jnp.asarray(np.stack(tabs).astype(np.float32))


def _dil_call(arr, col0, dilation):
    n, seq, _ = arr.shape
    kwin = min(DIL_KWIN, seq)
    group = max(1, min(n, DIL_STEP_TOKENS // seq))
    assert n % group == 0 and seq % DIL_QBLK == 0
    tab = _dil_tables(dilation, kwin)
    col = lambda off: pl.BlockSpec((group, seq, LANES), lambda j, i, off=off: (i, 0, off + j))
    out_spec = pl.BlockSpec((group, seq, LANES), lambda j, i: (i, 0, j))
    out_sds = jax.ShapeDtypeStruct((n, seq, DIL_WIDTH), jnp.float32)
    return pl.pallas_call(
        functools.partial(_dil_kernel, seq=seq, group=group, kwin=kwin),
        grid=(HEAD_PAIRS, n // group),
        in_specs=[
            col(col0), col(col0 + HEAD_PAIRS), col(col0 + 2 * HEAD_PAIRS),
            pl.BlockSpec((tab.shape[0], 2, DIL_QBLK, kwin), lambda j, i: (0, j, 0, 0)),
        ],
        out_specs=[out_spec, out_spec],
        out_shape=[out_sds, out_sds],
        scratch_shapes=_attn_scratch(min(group * seq // DIL_QBLK, ATTN_RING * DIL_BATCH), DIL_QBLK,
                                     kwin, (group, seq, LANES)),
        compiler_params=pltpu.CompilerParams(
            dimension_semantics=("parallel", "parallel"), vmem_limit_bytes=VMEM_LIMIT),
        name=f"dil_attn_d{dilation}",
    )(arr, arr, arr, tab)


def _tail_kernel(x_ref, oa_ref, o1_ref, l1_ref, o4_ref, l4_ref, o16_ref, l16_ref,
                 gna_ref, gdil_ref, wout_ref, gffn_ref, wg_ref, wu_ref, wd_ref, gfin_ref,
                 y_ref, o4s, l4s, o16s, l16s, tmp_ref, act_ref):
    tm = x_ref.shape[1]
    nslab = DIL_WIDTH // LANES
    r4, r16 = tm // 4, tm // 16
    for src, dst in ((o16_ref, o16s), (l16_ref, l16s)):
        for s in range(nslab):
            lanes = slice(s * LANES, (s + 1) * LANES)
            for w in range(4):
                for u in range(4):
                    tmp_ref[s, pl.ds(w * r4 + u, r16, stride=4), :] = src[0, 4 * w + u, :, lanes]
            for w in range(4):
                dst[s, pl.ds(w, r4, stride=4), :] = tmp_ref[s, w * r4:(w + 1) * r4, :]
    for src, dst in ((o4_ref, o4s), (l4_ref, l4s)):
        for s in range(nslab):
            for w in range(4):
                dst[s, pl.ds(w, r4, stride=4), :] = src[0, w, :, s * LANES:(s + 1) * LANES]

    od_parts = []
    for s in range(nslab):
        sl = slice(s * LANES, (s + 1) * LANES)
        l1, l4, l16 = l1_ref[0, :, sl], l4s[s], l16s[s]
        top = jnp.maximum(jnp.maximum(l1, l4), l16)
        w1, w4, w16 = jnp.exp(l1 - top), jnp.exp(l4 - top), jnp.exp(l16 - top)
        num = w1 * o1_ref[0, :, sl] + w4 * o4s[s] + w16 * o16s[s]
        od_parts.append(num / (w1 + w4 + w16))
    od = jnp.concatenate(od_parts, axis=-1)

    na = _rms(oa_ref[0], gna_ref[...]).astype(jnp.bfloat16)
    dn = _rms(od, gdil_ref[...]).astype(jnp.bfloat16)
    x2 = x_ref[0] + _dot(na, wout_ref[:NA_WIDTH, :]) + _dot(dn, wout_ref[NA_WIDTH:, :])

    hn = _rms(x2, gffn_ref[...]).astype(jnp.bfloat16)
    for c in range(D_FF // FFN_CHUNK):
        sl = slice(c * FFN_CHUNK, (c + 1) * FFN_CHUNK)
        gate = _dot(hn, wg_ref[:, sl])
        up = _dot(hn, wu_ref[:, sl])
        act_ref[:, sl] = (gate * jax.nn.sigmoid(gate) * up).astype(jnp.bfloat16)
    y_ref[0] = _rms(x2 + _dot(act_ref[...], wd_ref[...]), gfin_ref[...])


def _tail_call(x, oa, o1, l1, o4, l4, o16, l16, g_na, g_dil, w_out, g_ffn, w_gate, w_up, w_down,
               g_final):
    b, t, _ = x.shape
    tm = TAIL_TILE
    tok = lambda width: pl.BlockSpec((1, tm, width), lambda i, j: (i, j, 0))
    plane = lambda d: pl.BlockSpec((1, d, tm // d, DIL_WIDTH), lambda i, j: (i, 0, j, 0))
    const = lambda shape: pl.BlockSpec(shape, lambda i, j: (0,) * len(shape),
                                       pipeline_mode=pl.Buffered(1))
    slab = pltpu.VMEM((DIL_WIDTH // LANES, tm, LANES), jnp.float32)
    return pl.pallas_call(
        _tail_kernel,
        grid=(b, t // tm),
        in_specs=[
            tok(D_MODEL), tok(NA_WIDTH), tok(DIL_WIDTH), tok(DIL_WIDTH),
            plane(4), plane(4), plane(16), plane(16),
            const((1, NA_WIDTH)), const((1, DIL_WIDTH)), const((D_MODEL, D_MODEL)),
            const((1, D_MODEL)), const((D_MODEL, D_FF)), const((D_MODEL, D_FF)),
            const((D_FF, D_MODEL)), const((1, D_MODEL)),
        ],
        out_specs=tok(D_MODEL),
        out_shape=jax.ShapeDtypeStruct((b, t, D_MODEL), jnp.float32),
        scratch_shapes=[slab, slab, slab, slab, slab, pltpu.VMEM((tm, D_FF), jnp.bfloat16)],
        compiler_params=pltpu.CompilerParams(
            dimension_semantics=("parallel", "parallel"), vmem_limit_bytes=VMEM_LIMIT),
        name="tail",
    )(x, oa, o1, l1, o4, l4, o16, l16, g_na, g_dil, w_out, g_ffn, w_gate, w_up, w_down, g_final)


def _na_table(rpb):
    qc = np.arange(GRID_W)[:, None]
    kc = np.arange(GRID_W)[None, :]
    wstart = np.clip(qc - NA_COLS // 2, 0, GRID_W - NA_COLS)
    ok = (kc >= wstart) & (kc < wstart + NA_COLS)
    relc = np.clip(kc - qc + NA_COLS - 1, 0, 2 * NA_COLS - 2)
    onehot = (relc[None] == np.arange(2 * NA_COLS - 1)[:, None, None]).astype(np.float32)
    colbias = jnp.einsum("hrc,cqk->hrqk", rpb.astype(jnp.float32), jnp.asarray(onehot),
                         precision=lax.Precision.HIGHEST)
    colbias = jnp.where(jnp.asarray(ok)[None, None], colbias * LOG2E, NEG)
    variants = []
    for a in range(NA_ROWS):
        win = colbias[:, NA_ROWS - 1 - a:2 * NA_ROWS - 1 - a]
        variants.append(win.transpose(0, 2, 1, 3).reshape(NA_HEADS, GRID_W, NA_ROWS * GRID_W))
    return jnp.stack(variants, axis=1)


def _trunk(x, w_in, na_tab, g_attn, g_na, g_dil, w_out, g_ffn, w_gate, w_up, w_down, g_final):
    b, t, _ = x.shape
    nat, d4, d16 = _qkv_call(x, g_attn, w_in)
    oa = _na_call(nat, na_tab)
    o1, l1 = _dil_call(nat, (NA_WIDTH * 3) // LANES, 1)
    o4, l4 = _dil_call(d4.reshape(b * 4, t // 4, DIL_QKV), 0, 4)
    o16, l16 = _dil_call(d16.reshape(b * 16, t // 16, DIL_QKV), 0, 16)
    plane = lambda z, d: z.reshape(b, d, t // d, DIL_WIDTH)
    return _tail_call(x, oa, o1, l1, plane(o4, 4), plane(l4, 4), plane(o16, 16), plane(l16, 16),
                      g_na, g_dil, w_out, g_ffn, w_gate, w_up, w_down, g_final)


def kernel(x_prompt, x_sample, w_in, rpb, g_attn, g_na, g_dil, w_out, g_ffn, w_gate, w_up, w_down,
           g_final):
    assert w_in.shape[0] == 1, "single-layer trunk"
    bf = lambda w: w[0].astype(jnp.bfloat16)
    params = (bf(w_in), _na_table(rpb[0]), g_attn, g_na, g_dil, bf(w_out), g_ffn, bf(w_gate),
              bf(w_up), bf(w_down), g_final.reshape(1, D_MODEL))
    return _trunk(x_prompt, *params), _trunk(x_sample, *params)
```

```python
import functools

import jax
import jax.numpy as jnp
import numpy as np
from jax import lax
from jax.experimental import pallas as pl
from jax.experimental.pallas import tpu as pltpu

D_MODEL = 1024
HEAD_DIM = 64
NA_HEADS = 8
DIL_HEADS = 8
NA_WIDTH = NA_HEADS * HEAD_DIM
DIL_WIDTH = DIL_HEADS * HEAD_DIM
QKV_WIDTH = 3 * (NA_WIDTH + DIL_WIDTH)
DIL_QKV = 3 * DIL_WIDTH
D_FF = 2816
GRID_W = 64
NA_ROWS = 8
NA_COLS = 16
DIL_PATTERNS = ((128, 1), (512, 4), (2048, 16))
DIL_HALF = 64
RMS_EPS = 1e-6
NEG = -1e30
LOG2E = 1.4426950408889634
LN2 = 0.6931471805599453
Q_SCALE = HEAD_DIM ** -0.5 * LOG2E

LANES = 128
HEAD_PAIRS = NA_HEADS // 2
QKV_TILE = 512
TAIL_TILE = 512
DIL_QBLK = 128
DIL_KWIN = 256
DIL_STEP_TOKENS = 4096
FFN_CHUNK = 256
QKV_CHUNK = 256
NA_STEP_ROWS = 64
NA_BATCH = 4
DIL_BATCH = 4
ATTN_RING = 2
VMEM_LIMIT = 56 * 1024 * 1024

assert all(w // (2 * d) == DIL_HALF for w, d in DIL_PATTERNS)


def _rms(x, g):
    return x * lax.rsqrt(jnp.mean(x * x, axis=-1, keepdims=True) + RMS_EPS) * g


def _dot_nt(a, b):
    return lax.dot_general(a, b, (((1,), (1,)), ((), ())), preferred_element_type=jnp.float32)


def _dot(a, b):
    return jnp.dot(a, b, preferred_element_type=jnp.float32)


def _qkv_kernel(x_ref, g_ref, w_ref, nat_ref, d4_ref, d16_ref, slab_ref, slab4_ref, hn_ref):
    tm = x_ref.shape[1]
    hn_ref[...] = _rms(x_ref[0], g_ref[...]).astype(jnp.bfloat16)
    chunk = QKV_CHUNK
    slabs = chunk // LANES
    nchunk = QKV_WIDTH // chunk
    per_role = NA_WIDTH // chunk
    for c in [c for pair in zip(range(nchunk // 2, nchunk), range(nchunk // 2)) for c in pair]:
        acc = _dot(hn_ref[...], w_ref[:, c * chunk:(c + 1) * chunk])
        if (c // per_role) % 3 == 0:
            acc = acc * Q_SCALE
        nat_ref[0, :, c * chunk:(c + 1) * chunk] = acc.astype(jnp.bfloat16)
        if c < nchunk // 2:
            continue
        s0 = (c - nchunk // 2) * slabs
        for j in range(slabs):
            slab_ref[s0 + j] = acc[:, j * LANES:(j + 1) * LANES]
        r4, r16 = tm // 4, tm // 16
        for s in range(s0, s0 + slabs):
            lanes = slice(s * LANES, (s + 1) * LANES)
            for w in range(4):
                plane = slab_ref[s, pl.ds(w, r4, stride=4), :]
                d4_ref[0, w, :, lanes] = plane.astype(jnp.bfloat16)
                slab4_ref[s, w * r4:(w + 1) * r4, :] = plane
            for w in range(4):
                for u in range(4):
                    d16_ref[0, 4 * w + u, :, lanes] = (
                        slab4_ref[s, pl.ds(w * r4 + u, r16, stride=4), :].astype(jnp.bfloat16))


def _qkv_call(x, g, w):
    b, t, _ = x.shape
    tm = QKV_TILE
    return pl.pallas_call(
        _qkv_kernel,
        grid=(b, t // tm),
        in_specs=[
            pl.BlockSpec((1, tm, D_MODEL), lambda i, j: (i, j, 0)),
            pl.BlockSpec((1, D_MODEL), lambda i, j: (0, 0)),
            pl.BlockSpec((D_MODEL, QKV_WIDTH), lambda i, j: (0, 0), pipeline_mode=pl.Buffered(1)),
        ],
        out_specs=[
            pl.BlockSpec((1, tm, QKV_WIDTH), lambda i, j: (i, j, 0)),
            pl.BlockSpec((1, 4, tm // 4, DIL_QKV), lambda i, j: (i, 0, j, 0)),
            pl.BlockSpec((1, 16, tm // 16, DIL_QKV), lambda i, j: (i, 0, j, 0)),
        ],
        out_shape=[
            jax.ShapeDtypeStruct((b, t, QKV_WIDTH), jnp.bfloat16),
            jax.ShapeDtypeStruct((b, 4, t // 4, DIL_QKV), jnp.bfloat16),
            jax.ShapeDtypeStruct((b, 16, t // 16, DIL_QKV), jnp.bfloat16),
        ],
        scratch_shapes=[pltpu.VMEM((DIL_QKV // LANES, tm, LANES), jnp.float32),
                        pltpu.VMEM((DIL_QKV // LANES, tm, LANES), jnp.float32),
                        pltpu.VMEM((tm, D_MODEL), jnp.bfloat16)],
        compiler_params=pltpu.CompilerParams(
            dimension_semantics=("parallel", "parallel"), vmem_limit_bytes=VMEM_LIMIT),
        name="qkv_proj",
    )(x, g, w)


def _low_lanes(shape):
    return lax.broadcasted_iota(jnp.int32, shape, len(shape) - 1) < HEAD_DIM


def _fill_value_planes(v, vaug_ref):
    low = _low_lanes(v.shape)
    one = jnp.ones_like(v)
    vaug_ref[0] = jnp.where(low, v, one)
    vaug_ref[1] = jnp.where(low, one, v)


def _staged_attention(nblk, batch, stacked_scores, geom, load_q, load_k, load_v, bias, emit,
                      s_ref, mx_ref):
    m = s_ref.shape[1]
    kwin = s_ref.shape[2]
    low = _low_lanes((m, LANES))
    ring = s_ref.shape[0] // 2
    assert ring >= 2 * batch or ring >= nblk

    def scores(i):
        qidx, kidx, var = geom(i)
        qb = load_q(qidx)
        zero = jnp.zeros_like(qb)
        kb = load_k(kidx)
        qh = (jnp.where(low, qb, zero), jnp.where(low, zero, qb))
        if stacked_scores:
            s2 = _dot_nt(jnp.concatenate(qh, axis=0), kb)
            sh = (s2[:m], s2[m:])
        else:
            sh = (_dot_nt(qh[0], kb), _dot_nt(qh[1], kb))
        for h in range(2):
            u = 2 * (i % ring) + h
            s = sh[h] + bias(var, h)
            s_ref[u] = s
            mx_ref[u] = jnp.broadcast_to(jnp.max(s, axis=-1, keepdims=True), (m, LANES))

    def values(i):
        qidx, kidx, _ = geom(i)
        u0 = 2 * (i % ring)
        pv = []
        for h in range(2):
            mx = mx_ref[u0 + h]
            e = jnp.concatenate(
                [jnp.exp2(s_ref[u0 + h, :, c * LANES:(c + 1) * LANES] - mx)
                 for c in range(kwin // LANES)], axis=1)
            pv.append(_dot(e.astype(jnp.bfloat16), load_v(h, kidx)))
        num = jnp.where(low, pv[0], pv[1])
        den = pltpu.roll(jnp.where(low, pv[1], pv[0]), HEAD_DIM, 1)
        emit(qidx, num / den, jnp.where(low, mx_ref[u0], mx_ref[u0 + 1]), den)

    assert nblk % batch == 0
    nbatch = nblk // batch
    for b in range(nbatch + 1):
        for i in range(batch):
            if b < nbatch:
                scores(b * batch + i)
            if b >= 1:
                values((b - 1) * batch + i)


def _attn_scratch(ring, m, kwin, vaug_shape):
    return [
        pltpu.VMEM((2 * ring, m, kwin), jnp.float32),
        pltpu.VMEM((2 * ring, m, LANES), jnp.float32),
        pltpu.VMEM((2,) + vaug_shape, jnp.bfloat16),
    ]


def _na_kernel(q_ref, k_ref, v_ref, tab_ref, o_ref, s_ref, mx_ref, vaug_ref, *, rows):
    _fill_value_planes(v_ref[...], vaug_ref)

    def geom(i):
        g, r = i // rows, i % rows
        rs = min(max(r - NA_ROWS // 2, 0), rows - NA_ROWS)
        return (g, r * GRID_W), (g, rs * GRID_W), r - rs

    def emit(qidx, out, mx, den):
        o_ref[qidx[0], pl.ds(qidx[1], GRID_W), :] = out

    _staged_attention(
        q_ref.shape[0] * rows, NA_BATCH, True, geom,
        load_q=lambda qidx: q_ref[qidx[0], pl.ds(qidx[1], GRID_W), :],
        load_k=lambda kidx: k_ref[kidx[0], pl.ds(kidx[1], NA_ROWS * GRID_W), :],
        load_v=lambda h, kidx: vaug_ref[h, kidx[0], pl.ds(kidx[1], NA_ROWS * GRID_W), :],
        bias=lambda var, h: tab_ref[h, var],
        emit=emit, s_ref=s_ref, mx_ref=mx_ref)


def _na_call(nat, tab):
    b, t, _ = nat.shape
    rows = t // GRID_W
    kw = NA_ROWS * GRID_W
    group = max(1, min(b, NA_STEP_ROWS // rows))
    assert b % group == 0
    col = lambda off: pl.BlockSpec((group, t, LANES), lambda j, i, off=off: (i, 0, off + j))
    return pl.pallas_call(
        functools.partial(_na_kernel, rows=rows),
        grid=(HEAD_PAIRS, b // group),
        in_specs=[
            col(0), col(HEAD_PAIRS), col(2 * HEAD_PAIRS),
            pl.BlockSpec((2, NA_ROWS, GRID_W, kw), lambda j, i: (j, 0, 0, 0)),
        ],
        out_specs=pl.BlockSpec((group, t, LANES), lambda j, i: (i, 0, j)),
        out_shape=jax.ShapeDtypeStruct((b, t, NA_WIDTH), jnp.float32),
        scratch_shapes=_attn_scratch(min(group * rows, ATTN_RING * NA_BATCH), GRID_W, kw,
                                     (group, t, LANES)),
        compiler_params=pltpu.CompilerParams(
            dimension_semantics=("parallel", "parallel"), vmem_limit_bytes=VMEM_LIMIT),
        name="na_attn",
    )(nat, nat, nat, tab)


def _dil_kernel(q_ref, k_ref, v_ref, tab_ref, o_ref, lse_ref, s_ref, mx_ref, vaug_ref,
                *, seq, group, kwin):
    nblk = seq // DIL_QBLK
    _fill_value_planes(v_ref[...], vaug_ref)

    def geom(i):
        g, q0 = i // nblk, (i % nblk) * DIL_QBLK
        k0 = min(max(q0 - DIL_HALF, 0), seq - kwin)
        return (g, q0), (g, k0), (q0 - k0) // DIL_HALF

    def emit(qidx, out, mx, den):
        g, q0 = qidx
        o_ref[g, pl.ds(q0, DIL_QBLK), :] = out
        lse_ref[g, pl.ds(q0, DIL_QBLK), :] = mx * LN2 + jnp.log(den)

    _staged_attention(
        group * nblk, DIL_BATCH, False, geom,
        load_q=lambda qidx: q_ref[qidx[0], pl.ds(qidx[1], DIL_QBLK), :],
        load_k=lambda kidx: k_ref[kidx[0], pl.ds(kidx[1], kwin), :],
        load_v=lambda h, kidx: vaug_ref[h, kidx[0], pl.ds(kidx[1], kwin), :],
        bias=lambda var, h: tab_ref[var, h],
        emit=emit, s_ref=s_ref, mx_ref=mx_ref)


def _dil_tables(dilation, kwin):
    slopes = np.array([2.0 ** (-8.0 * (i + 1) / DIL_HEADS) for i in range(DIL_HEADS)], np.float32)
    i = np.arange(DIL_QBLK)[:, None]
    j = np.arange(kwin)[None, :]
    tabs = []
    for v in range((kwin - DIL_QBLK) // DIL_HALF + 1):
        diff = np.abs(j - i - v * DIL_HALF)
        dist = (dilation * diff).astype(np.float32)
        bias = -slopes[:, None, None] * dist[None]
        tabs.append(np.where((diff <= DIL_HALF)[None], bias * np.float32(LOG2E), np.float32(NEG)))
    return jnp.asarray(np.stack(tabs).astype(np.float32))


def _dil_call(arr, col0, dilation):
    n, seq, _ = arr.shape
    kwin = min(DIL_KWIN, seq)
    group = max(1, min(n, DIL_STEP_TOKENS // seq))
    assert n % group == 0 and seq % DIL_QBLK == 0
    tab = _dil_tables(dilation, kwin)
    col = lambda off: pl.BlockSpec((group, seq, LANES), lambda j, i, off=off: (i, 0, off + j))
    out_spec = pl.BlockSpec((group, seq, LANES), lambda j, i: (i, 0, j))
    out_sds = jax.ShapeDtypeStruct((n, seq, DIL_WIDTH), jnp.float32)
    return pl.pallas_call(
        functools.partial(_dil_kernel, seq=seq, group=group, kwin=kwin),
        grid=(HEAD_PAIRS, n // group),
        in_specs=[
            col(col0), col(col0 + HEAD_PAIRS), col(col0 + 2 * HEAD_PAIRS),
            pl.BlockSpec((tab.shape[0], 2, DIL_QBLK, kwin), lambda j, i: (0, j, 0, 0)),
        ],
        out_specs=[out_spec, out_spec],
        out_shape=[out_sds, out_sds],
        scratch_shapes=_attn_scratch(min(group * seq // DIL_QBLK, ATTN_RING * DIL_BATCH), DIL_QBLK,
                                     kwin, (group, seq, LANES)),
        compiler_params=pltpu.CompilerParams(
            dimension_semantics=("parallel", "parallel"), vmem_limit_bytes=VMEM_LIMIT),
        name=f"dil_attn_d{dilation}",
    )(arr, arr, arr, tab)


def _tail_kernel(x_ref, oa_ref, o1_ref, l1_ref, o4_ref, l4_ref, o16_ref, l16_ref,
                 gna_ref, gdil_ref, wout_ref, gffn_ref, wg_ref, wu_ref, wd_ref, gfin_ref,
                 y_ref, o4s, l4s, o16s, l16s, tmp_ref, act_ref):
    tm = x_ref.shape[1]
    nslab = DIL_WIDTH // LANES
    r4, r16 = tm // 4, tm // 16
    for src, dst in ((o16_ref, o16s), (l16_ref, l16s)):
        for s in range(nslab):
            lanes = slice(s * LANES, (s + 1) * LANES)
            for w in range(4):
                for u in range(4):
                    tmp_ref[s, pl.ds(w * r4 + u, r16, stride=4), :] = src[0, 4 * w + u, :, lanes]
            for w in range(4):
                dst[s, pl.ds(w, r4, stride=4), :] = tmp_ref[s, w * r4:(w + 1) * r4, :]
    for src, dst in ((o4_ref, o4s), (l4_ref, l4s)):
        for s in range(nslab):
            for w in range(4):
                dst[s, pl.ds(w, r4, stride=4), :] = src[0, w, :, s * LANES:(s + 1) * LANES]

    od_parts = []
    for s in range(nslab):
        sl = slice(s * LANES, (s + 1) * LANES)
        l1, l4, l16 = l1_ref[0, :, sl], l4s[s], l16s[s]
        top = jnp.maximum(jnp.maximum(l1, l4), l16)
        w1, w4, w16 = jnp.exp(l1 - top), jnp.exp(l4 - top), jnp.exp(l16 - top)
        num = w1 * o1_ref[0, :, sl] + w4 * o4s[s] + w16 * o16s[s]
        od_parts.append(num / (w1 + w4 + w16))
    od = jnp.concatenate(od_parts, axis=-1)

    na = _rms(oa_ref[0], gna_ref[...]).astype(jnp.bfloat16)
    dn = _rms(od, gdil_ref[...]).astype(jnp.bfloat16)
    x2 = x_ref[0] + _dot(na, wout_ref[:NA_WIDTH, :]) + _dot(dn, wout_ref[NA_WIDTH:, :])

    hn = _rms(x2, gffn_ref[...]).astype(jnp.bfloat16)
    for c in range(D_FF // FFN_CHUNK):
        sl = slice(c * FFN_CHUNK, (c + 1) * FFN_CHUNK)
        gate = _dot(hn, wg_ref[:, sl])
        up = _dot(hn, wu_ref[:, sl])
        act_ref[:, sl] = (gate * jax.nn.sigmoid(gate) * up).astype(jnp.bfloat16)
    y_ref[0] = _rms(x2 + _dot(act_ref[...], wd_ref[...]), gfin_ref[...])


def _tail_call(x, oa, o1, l1, o4, l4, o16, l16, g_na, g_dil, w_out, g_ffn, w_gate, w_up, w_down,
               g_final):
    b, t, _ = x.shape
    tm = TAIL_TILE
    tok = lambda width: pl.BlockSpec((1, tm, width), lambda i, j: (i, j, 0))
    plane = lambda d: pl.BlockSpec((1, d, tm // d, DIL_WIDTH), lambda i, j: (i, 0, j, 0))
    const = lambda shape: pl.BlockSpec(shape, lambda i, j: (0,) * len(shape),
                                       pipeline_mode=pl.Buffered(1))
    slab = pltpu.VMEM((DIL_WIDTH // LANES, tm, LANES), jnp.float32)
    return pl.pallas_call(
        _tail_kernel,
        grid=(b, t // tm),
        in_specs=[
            tok(D_MODEL), tok(NA_WIDTH), tok(DIL_WIDTH), tok(DIL_WIDTH),
            plane(4), plane(4), plane(16), plane(16),
            const((1, NA_WIDTH)), const((1, DIL_WIDTH)), const((D_MODEL, D_MODEL)),
            const((1, D_MODEL)), const((D_MODEL, D_FF)), const((D_MODEL, D_FF)),
            const((D_FF, D_MODEL)), const((1, D_MODEL)),
        ],
        out_specs=tok(D_MODEL),
        out_shape=jax.ShapeDtypeStruct((b, t, D_MODEL), jnp.float32),
        scratch_shapes=[slab, slab, slab, slab, slab, pltpu.VMEM((tm, D_FF), jnp.bfloat16)],
        compiler_params=pltpu.CompilerParams(
            dimension_semantics=("parallel", "parallel"), vmem_limit_bytes=VMEM_LIMIT),
        name="tail",
    )(x, oa, o1, l1, o4, l4, o16, l16, g_na, g_dil, w_out, g_ffn, w_gate, w_up, w_down, g_final)


def _na_table(rpb):
    qc = np.arange(GRID_W)[:, None]
    kc = np.arange(GRID_W)[None, :]
    wstart = np.clip(qc - NA_COLS // 2, 0, GRID_W - NA_COLS)
    ok = (kc >= wstart) & (kc < wstart + NA_COLS)
    relc = np.clip(kc - qc + NA_COLS - 1, 0, 2 * NA_COLS - 2)
    onehot = (relc[None] == np.arange(2 * NA_COLS - 1)[:, None, None]).astype(np.float32)
    colbias = jnp.einsum("hrc,cqk->hrqk", rpb.astype(jnp.float32), jnp.asarray(onehot),
                         precision=lax.Precision.HIGHEST)
    colbias = jnp.where(jnp.asarray(ok)[None, None], colbias * LOG2E, NEG)
    variants = []
    for a in range(NA_ROWS):
        win = colbias[:, NA_ROWS - 1 - a:2 * NA_ROWS - 1 - a]
        variants.append(win.transpose(0, 2, 1, 3).reshape(NA_HEADS, GRID_W, NA_ROWS * GRID_W))
    return jnp.stack(variants, axis=1)


def _trunk(x, w_in, na_tab, g_attn, g_na, g_dil, w_out, g_ffn, w_gate, w_up, w_down, g_final):
    b, t, _ = x.shape
    nat, d4, d16 = _qkv_call(x, g_attn, w_in)
    oa = _na_call(nat, na_tab)
    o1, l1 = _dil_call(nat, (NA_WIDTH * 3) // LANES, 1)
    o4, l4 = _dil_call(d4.reshape(b * 4, t // 4, DIL_QKV), 0, 4)
    o16, l16 = _dil_call(d16.reshape(b * 16, t // 16, DIL_QKV), 0, 16)
    plane = lambda z, d: z.reshape(b, d, t // d, DIL_WIDTH)
    return _tail_call(x, oa, o1, l1, plane(o4, 4), plane(l4, 4), plane(o16, 16), plane(l16, 16),
                      g_na, g_dil, w_out, g_ffn, w_gate, w_up, w_down, g_final)


def kernel(x_prompt, x_sample, w_in, rpb, g_attn, g_na, g_dil, w_out, g_ffn, w_gate, w_up, w_down,
           g_final):
    assert w_in.shape[0] == 1, "single-layer trunk"
    bf = lambda w: w[0].astype(jnp.bfloat16)
    params = (bf(w_in), _na_table(rpb[0]), g_attn, g_na, g_dil, bf(w_out), g_ffn, bf(w_gate),
              bf(w_up), bf(w_down), g_final.reshape(1, D_MODEL))
    return _trunk(x_prompt, *params), _trunk(x_sample, *params)
```
